```python
import jax, jax.numpy as jnp
from jax import lax
import numpy as np

D_MODEL = 1024
BATCH = 8
SEQ = 2048
DEPTH = 4

N_REC_LAYERS = (DEPTH + 1) // 2
N_ATT_LAYERS = DEPTH // 2
D_A = D_MODEL // 2
A_HEADS = 4
A_BLOCK = D_A // A_HEADS
CONV_WIDTH = 4
CONV_PAD = (2, 1)
RGLRU_C = 8.0
D_B = D_MODEL // 2
B_HEADS = 4
B_DK = D_B // B_HEADS
B_DV = D_B // B_HEADS
CHUNK = 32
REC_IN = 2 * D_A + 5 * D_B
REC_SPLITS = (D_A, 2 * D_A, 2 * D_A + D_B, 2 * D_A + 2 * D_B, 2 * D_A + 3 * D_B, 2 * D_A + 4 * D_B)
HEAD_DIM = 64
N_Q_HEADS = D_MODEL // HEAD_DIM
N_KV_HEADS = 4
GROUP = N_Q_HEADS // N_KV_HEADS
WINDOW = 128
QBLOCK = 128
ROPE_THETA = 10000.0
QKV_OUT = (N_Q_HEADS + 2 * N_KV_HEADS) * HEAD_DIM
D_FF = 4 * D_MODEL
EPS = 1e-6

kernel_name = "hybrid_rglru_hgrn2_swa_encoder"


def rmsnorm(x, g):
    xf = x.astype(jnp.float32)
    y = xf * lax.rsqrt(jnp.mean(xf * xf, axis=-1, keepdims=True) + EPS) * g.astype(jnp.float32)
    return y.astype(x.dtype)


def dwconv(x, w, b):
    y = lax.conv_general_dilated(x, w[:, None, :].astype(x.dtype), window_strides=(1,),
                                 padding=[CONV_PAD], dimension_numbers=('NWC', 'WIO', 'NWC'),
                                 feature_group_count=x.shape[-1])
    return y + b.astype(x.dtype)


def block_diag(x, w, b):
    xh = x.reshape(x.shape[:-1] + (A_HEADS, A_BLOCK))
    return jnp.einsum('bshi,hij->bshj', xh, w.astype(jnp.float32)).reshape(x.shape) + b.astype(jnp.float32)


def rglru(xc, w_r, b_r, w_i, b_i, lam, reverse):
    r = jax.nn.sigmoid(block_diag(xc, w_r, b_r))
    i = jax.nn.sigmoid(block_diag(xc, w_i, b_i))
    log_a = -RGLRU_C * r * jax.nn.softplus(-lam.astype(jnp.float32))
    a = jnp.exp(log_a)
    u = jnp.sqrt(-jnp.expm1(2.0 * log_a)) * (i * xc)

    def combine(left, right):
        a1, b1 = left
        a2, b2 = right
        return a1 * a2, a2 * b1 + b2

    _, h = lax.associative_scan(combine, (a, u), axis=1, reverse=reverse)
    return h


def hgrn2_direction(q, log_f, v):
    bsz, s = q.shape[:2]
    n = s // CHUNK

    def chunks(t):
        return t.reshape(bsz, n, CHUNK, B_HEADS, t.shape[-1]).transpose(0, 3, 1, 2, 4)

    q, log_f, v = chunks(q), chunks(log_f), chunks(v)
    k = -jnp.expm1(log_f)
    bcum = jnp.cumsum(log_f, axis=3)
    b_last = bcum[:, :, :, -1:, :]
    qe = q * jnp.exp(bcum)
    ke = k * jnp.exp(-bcum)
    tri = jnp.tril(jnp.ones((CHUNK, CHUNK), dtype=bool))
    att = jnp.where(tri, jnp.einsum('bhncd,bhned->bhnce', qe, ke), 0.0)
    o = jnp.einsum('bhnce,bhnev->bhncv', att, v)
    kd = k * jnp.exp(b_last - bcum)
    upd = jnp.einsum('bhncd,bhncv->nbhdv', kd, v)
    dec = jnp.exp(b_last[:, :, :, 0, :]).transpose(2, 0, 1, 3)

    def step(state, inp):
        d, u = inp
        return d[..., None] * state + u, state

    init = jnp.zeros((bsz, B_HEADS, B_DK, B_DV), q.dtype)
    _, s_prev = lax.scan(step, init, (dec, upd))
    o = o + jnp.einsum('bhncd,nbhdv->bhncv', qe, s_prev)
    return o.transpose(0, 2, 3, 1, 4).reshape(bsz, s, B_HEADS, B_DV)


def rec_mixer(u, w_in, conv_w, conv_b, w_r, b_r, w_i, b_i, lam, lb, norm_g, w_out):
    bsz, s, _ = u.shape
    proj = u @ w_in
    xa, ga, q, zf, zb, iv, g = jnp.split(proj, REC_SPLITS, axis=-1)
    xc = dwconv(xa, conv_w, conv_b).astype(jnp.float32)
    y_rec = (rglru(xc, w_r[0], b_r[0], w_i[0], b_i[0], lam[0], False)
             + rglru(xc, w_r[1], b_r[1], w_i[1], b_i[1], lam[1], True))
    y_a = jax.nn.gelu(ga.astype(jnp.float32)) * y_rec
    heads = lambda t: t.astype(jnp.float32).reshape(bsz, s, B_HEADS, -1)
    lbf = lb.astype(jnp.float32)
    log_ff = jnp.log(lbf[0] + (1.0 - lbf[0]) * jax.nn.sigmoid(zf.astype(jnp.float32)))
    log_fb = jnp.log(lbf[1] + (1.0 - lbf[1]) * jax.nn.sigmoid(zb.astype(jnp.float32)))
    qh, vh = heads(q), heads(iv)
    flip = lambda t: jnp.flip(t, axis=1)
    o = (hgrn2_direction(qh, heads(log_ff), vh)
         + flip(hgrn2_direction(flip(qh), flip(heads(log_fb)), flip(vh))))
    o = o * lax.rsqrt(jnp.mean(o * o, axis=-1, keepdims=True) + EPS)
    o = o * norm_g.astype(jnp.float32).reshape(B_HEADS, B_DV)
    y_b = o.reshape(bsz, s, D_B) * jax.nn.silu(g.astype(jnp.float32))
    y = jnp.concatenate([y_a, y_b], axis=-1).astype(u.dtype)
    return y @ w_out


def rope(t, cos, sin):
    half = t.shape[-1] // 2
    t1, t2 = t[..., :half], t[..., half:]
    return jnp.concatenate([t1 * cos - t2 * sin, t2 * cos + t1 * sin], axis=-1)


def window_attention(u, w_qkv, sinks, w_o, cos, sin, band_mask):
    bsz, s, _ = u.shape
    nb = s // QBLOCK
    qkv = u @ w_qkv
    q, k, v = jnp.split(qkv, (N_Q_HEADS * HEAD_DIM, (N_Q_HEADS + N_KV_HEADS) * HEAD_DIM), axis=-1)
    q = rope(q.reshape(bsz, s, N_Q_HEADS, HEAD_DIM), cos, sin)
    k = rope(k.reshape(bsz, s, N_KV_HEADS, HEAD_DIM), cos, sin)
    v = v.reshape(bsz, s, N_KV_HEADS, HEAD_DIM)

    def band(t):
        tp = jnp.pad(t, ((0, 0), (QBLOCK, QBLOCK), (0, 0), (0, 0)))
        tp = tp.reshape(bsz, nb + 2, QBLOCK, N_KV_HEADS, HEAD_DIM)
        return jnp.concatenate([tp[:, :-2], tp[:, 1:-1], tp[:, 2:]], axis=2)

    kb, vb = band(k), band(v)
    qb = q.reshape(bsz, nb, QBLOCK, N_KV_HEADS, GROUP, HEAD_DIM)
    sc = jnp.einsum('bnqhgd,bnkhd->bnhgqk', qb, kb).astype(jnp.float32) * (HEAD_DIM ** -0.5)
    sc = jnp.where(band_mask[None, :, None, None], sc, -jnp.inf)
    sink = jnp.broadcast_to(sinks.astype(jnp.float32).reshape(1, 1, N_KV_HEADS, GROUP, 1, 1),
                            sc.shape[:-1] + (1,))
    p = jax.nn.softmax(jnp.concatenate([sc, sink], axis=-1), axis=-1)[..., :-1]
    o = jnp.einsum('bnhgqk,bnkhd->bnqhgd', p.astype(vb.dtype), vb)
    return o.reshape(bsz, s, N_Q_HEADS * HEAD_DIM) @ w_o


def sqrelu_mlp(u, w1, w2):
    hdn = jnp.square(jax.nn.relu(u @ w1))
    return hdn @ w2


def setup_inputs(seed: int = 0) -> dict:
    key = jax.random.key(seed)
    ks = jax.random.split(key, 20)
    nrm = lambda k, shape, scale: jax.random.normal(k, shape, jnp.float32) * scale
    u = jax.random.uniform(ks[10], (N_REC_LAYERS, 2, D_A), jnp.float32, 0.9, 0.999)
    a0 = u ** (1.0 / RGLRU_C)
    rg_lambda = jnp.log(a0) - jnp.log1p(-a0)
    return {
        "x": nrm(ks[0], (BATCH, SEQ, D_MODEL), 1.0),
        "norm_g": 1.0 + nrm(ks[1], (DEPTH, 4, D_MODEL), 0.02),
        "rec_w_in": nrm(ks[2], (N_REC_LAYERS, D_MODEL, REC_IN), D_MODEL ** -0.5),
        "rec_conv_w": nrm(ks[3], (N_REC_LAYERS, CONV_WIDTH, D_A), CONV_WIDTH ** -0.5),
        "rec_conv_b": nrm(ks[4], (N_REC_LAYERS, D_A), 0.01),
        "rg_w_r": nrm(ks[5], (N_REC_LAYERS, 2, A_HEADS, A_BLOCK, A_BLOCK), A_BLOCK ** -0.5),
        "rg_b_r": nrm(ks[6], (N_REC_LAYERS, 2, D_A), 0.01),
        "rg_w_i": nrm(ks[7], (N_REC_LAYERS, 2, A_HEADS, A_BLOCK, A_BLOCK), A_BLOCK ** -0.5),
        "rg_b_i": nrm(ks[8], (N_REC_LAYERS, 2, D_A), 0.01),
        "rg_lambda": rg_lambda,
        "hgrn_lb_logits": nrm(ks[11], (2, N_REC_LAYERS, D_B), 0.5),
        "hgrn_norm_g": 1.0 + nrm(ks[12], (N_REC_LAYERS, D_B), 0.02),
        "rec_w_out": nrm(ks[13], (N_REC_LAYERS, D_A + D_B, D_MODEL), (D_A + D_B) ** -0.5),
        "att_w_qkv": nrm(ks[14], (N_ATT_LAYERS, D_MODEL, QKV_OUT), D_MODEL ** -0.5),
        "att_sinks": nrm(ks[15], (N_ATT_LAYERS, N_Q_HEADS), 0.5),
        "att_w_o": nrm(ks[16], (N_ATT_LAYERS, N_Q_HEADS * HEAD_DIM, D_MODEL), (N_Q_HEADS * HEAD_DIM) ** -0.5),
        "mlp_w1": nrm(ks[17], (DEPTH, D_MODEL, D_FF), D_MODEL ** -0.5),
        "mlp_w2": nrm(ks[18], (DEPTH, D_FF, D_MODEL), D_FF ** -0.5),
    }


def reference(x, norm_g, rec_w_in, rec_conv_w, rec_conv_b, rg_w_r, rg_b_r, rg_w_i, rg_b_i,
              rg_lambda, hgrn_lb_logits, hgrn_norm_g, rec_w_out, att_w_qkv, att_sinks, att_w_o,
              mlp_w1, mlp_w2):
    s = x.shape[1]
    nb = s // QBLOCK
    pos = jnp.arange(s, dtype=jnp.float32)
    inv_freq = ROPE_THETA ** (-jnp.arange(0, HEAD_DIM, 2, dtype=jnp.float32) / HEAD_DIM)
    ang = pos[:, None] * inv_freq[None, :]
    cos = jnp.cos(ang)[:, None, :].astype(x.dtype)
    sin = jnp.sin(ang)[:, None, :].astype(x.dtype)
    qpos = jnp.arange(nb)[:, None, None] * QBLOCK + jnp.arange(QBLOCK)[None, :, None]
    kpos = (jnp.arange(nb)[:, None, None] - 1) * QBLOCK + jnp.arange(3 * QBLOCK)[None, None, :]
    band_mask = (jnp.abs(qpos - kpos) <= WINDOW) & (kpos >= 0) & (kpos < s)
    s_lb = jax.nn.softmax(hgrn_lb_logits.astype(jnp.float32), axis=1)
    lbs = jnp.cumsum(s_lb, axis=1) - s_lb[:, :1]

    h = x
    for layer in range(DEPTH):
        g = norm_g[layer]
        un = rmsnorm(h, g[0])
        if layer % 2 == 0:
            r = layer // 2
            m = rec_mixer(un, rec_w_in[r], rec_conv_w[r], rec_conv_b[r], rg_w_r[r], rg_b_r[r],
                          rg_w_i[r], rg_b_i[r], rg_lambda[r], lbs[:, r], hgrn_norm_g[r], rec_w_out[r])
        else:
            a = layer // 2
            m = window_attention(un, att_w_qkv[a], att_sinks[a], att_w_o[a], cos, sin, band_mask)
        h = h + rmsnorm(m.astype(h.dtype), g[1])
        m = sqrelu_mlp(rmsnorm(h, g[2]), mlp_w1[layer], mlp_w2[layer])
        h = h + rmsnorm(m.astype(h.dtype), g[3])
    return h
```

```python
import functools

import jax
import jax.numpy as jnp
from jax import lax
from jax.experimental import pallas as pl
from jax.experimental.pallas import tpu as pltpu

D_MODEL = 1024
DEPTH = 4
N_REC_LAYERS = (DEPTH + 1) // 2
D_A = D_MODEL // 2
A_HEADS = 4
A_BLOCK = D_A // A_HEADS
CONV_WIDTH = 4
RGLRU_C = 8.0
D_B = D_MODEL // 2
B_HEADS = 4
B_DK = D_B // B_HEADS
HGRN_CHUNK = 32
REC_IN = 2 * D_A + 5 * D_B
HEAD_DIM = 64
N_Q_HEADS = D_MODEL // HEAD_DIM
N_KV_HEADS = 4
GROUP = N_Q_HEADS // N_KV_HEADS
WINDOW = 128
QBLOCK = 128
ROPE_THETA = 10000.0
QKV_OUT = (N_Q_HEADS + 2 * N_KV_HEADS) * HEAD_DIM
D_FF = 4 * D_MODEL
EPS = 1e-6

LANES = 128
SUBLANES = 8
ROW_TILE = 1024
FF_TILE = 1024
MIX_ROWS = 256
VMEM_LIMIT = 52 * 1024 * 1024

BF16 = jnp.bfloat16
F32 = jnp.float32


def _rms(x, g):
    return x * lax.rsqrt(jnp.mean(x * x, axis=-1, keepdims=True) + EPS) * g


def _dot(a, b):
    return jnp.dot(a, b, preferred_element_type=F32)


def _dot_nt(a, b):
    return lax.dot_general(a, b, (((1,), (1,)), ((), ())), preferred_element_type=F32)


def _dot_tn(a, b):
    return lax.dot_general(a, b, (((0,), (0,)), ((), ())), preferred_element_type=F32)


def _norm_matmul_kernel(x_ref, g_ref, w_ref, o_ref, xn_ref):
    @pl.when(pl.program_id(1) == 0)
    def _():
        xn_ref[...] = _rms(x_ref[...], g_ref[...]).astype(BF16)

    o_ref[...] = _dot(xn_ref[...], w_ref[...]).astype(o_ref.dtype)


def _norm_matmul(x, g, w, col_tile, out_dtype):
    m, k = x.shape
    n = w.shape[1]
    return pl.pallas_call(
        _norm_matmul_kernel,
        grid=(m // ROW_TILE, n // col_tile),
        in_specs=[
            pl.BlockSpec((ROW_TILE, k), lambda i, j: (i, 0)),
            pl.BlockSpec((1, k), lambda i, j: (0, 0)),
            pl.BlockSpec((k, col_tile), lambda i, j: (0, j)),
        ],
        out_specs=pl.BlockSpec((ROW_TILE, col_tile), lambda i, j: (i, j)),
        out_shape=jax.ShapeDtypeStruct((m, n), out_dtype),
        scratch_shapes=[pltpu.VMEM((ROW_TILE, k), BF16)],
        compiler_params=pltpu.CompilerParams(
            dimension_semantics=("parallel", "arbitrary"), vmem_limit_bytes=VMEM_LIMIT),
        name="norm_matmul",
    )(x, g.reshape(1, k), w)


def _matmul_norm_res_kernel(a_ref, w_ref, g_ref, h_ref, o_ref):
    m = _dot(a_ref[...], w_ref[...])
    o_ref[...] = h_ref[...] + _rms(m, g_ref[...])


def _matmul_norm_res(a, w, g, h):
    m, k = a.shape
    n = w.shape[1]
    return pl.pallas_call(
        _matmul_norm_res_kernel,
        grid=(m // ROW_TILE,),
        in_specs=[
            pl.BlockSpec((ROW_TILE, k), lambda i: (i, 0)),
            pl.BlockSpec((k, n), lambda i: (0, 0)),
            pl.BlockSpec((1, n), lambda i: (0, 0)),
            pl.BlockSpec((ROW_TILE, n), lambda i: (i, 0)),
        ],
        out_specs=pl.BlockSpec((ROW_TILE, n), lambda i: (i, 0)),
        out_shape=jax.ShapeDtypeStruct((m, n), F32),
        compiler_params=pltpu.CompilerParams(
            dimension_semantics=("parallel",), vmem_limit_bytes=VMEM_LIMIT),
        name="matmul_norm_res",
    )(a, w, g.reshape(1, n), h)


def _mlp_kernel(h_ref, g_in_ref, w1_ref, w2_ref, g_out_ref, o_ref, xn_ref, acc_ref):
    j = pl.program_id(1)

    @pl.when(j == 0)
    def _():
        xn_ref[...] = _rms(h_ref[...], g_in_ref[...]).astype(BF16)
        acc_ref[...] = jnp.zeros_like(acc_ref)

    hid = jnp.maximum(_dot(xn_ref[...], w1_ref[...]), 0.0)
    acc_ref[...] += _dot((hid * hid).astype(BF16), w2_ref[...])

    @pl.when(j == pl.num_programs(1) - 1)
    def _():
        o_ref[...] = h_ref[...] + _rms(acc_ref[...], g_out_ref[...])


def _mlp(h, g_in, w1, w2, g_out):
    m, d = h.shape
    ff = w1.shape[1]
    return pl.pallas_call(
        _mlp_kernel,
        grid=(m // ROW_TILE, ff // FF_TILE),
        in_specs=[
            pl.BlockSpec((ROW_TILE, d), lambda i, j: (i, 0)),
            pl.BlockSpec((1, d), lambda i, j: (0, 0)),
            pl.BlockSpec((d, FF_TILE), lambda i, j: (0, j)),
            pl.BlockSpec((FF_TILE, d), lambda i, j: (j, 0)),
            pl.BlockSpec((1, d), lambda i, j: (0, 0)),
        ],
        out_specs=pl.BlockSpec((ROW_TILE, d), lambda i, j: (i, 0)),
        out_shape=jax.ShapeDtypeStruct((m, d), F32),
        scratch_shapes=[pltpu.VMEM((ROW_TILE, d), BF16), pltpu.VMEM((ROW_TILE, d), F32)],
        compiler_params=pltpu.CompilerParams(
            dimension_semantics=("parallel", "arbitrary"), vmem_limit_bytes=VMEM_LIMIT),
        name="mlp",
    )(h, g_in.reshape(1, d), w1, w2, g_out.reshape(1, d))


def _row_ids(rows):
    return lax.broadcasted_iota(jnp.int32, (rows, LANES), 0)


def _shift_down(x, k):
    return pltpu.roll(x, k, 0)


def _shift_up(x, k):
    return pltpu.roll(x, x.shape[0] - k, 0)


def _seg_cumsum(x, seg, reverse):
    pos = _row_ids(x.shape[0]) % seg
    k = 1
    while k < seg:
        if reverse:
            x = x + jnp.where(pos < seg - k, _shift_up(x, k), 0.0)
        else:
            x = x + jnp.where(pos >= k, _shift_down(x, k), 0.0)
        k *= 2
    return x


def _seg_linear_scan(a, u, seg, reverse):
    pos = _row_ids(a.shape[0]) % seg
    k = 1
    while k < seg:
        if reverse:
            keep = pos < seg - k
            a_n, u_n = _shift_up(a, k), _shift_up(u, k)
        else:
            keep = pos >= k
            a_n, u_n = _shift_down(a, k), _shift_down(u, k)
        u = u + jnp.where(keep, a * u_n, 0.0)
        a = a * jnp.where(keep, a_n, 1.0)
        k *= 2
    return a, u


def _softplus(x):
    return jnp.maximum(x, 0.0) + jnp.log1p(jnp.exp(-jnp.abs(x)))


def _gelu_tanh(x):
    return 0.5 * x * (1.0 + jnp.tanh(0.7978845608028654 * (x + 0.044715 * (x * x * x))))


def _rec_mixer_kernel(layer, seq,
                      xa_ref, ga_ref, q_ref, zf_ref, zb_ref, iv_ref, g_ref,
                      cw_ref, cb_ref, wg_ref, bg_ref, lam_ref, lbl_ref, ng_ref,
                      y_ref,
                      xpad_ref, p_ref, l_ref, qe_ref, kd_ref, v_ref, o_ref, tot_ref, st_ref):
    n_blocks = seq // MIX_ROWS
    n_chunks = seq // HGRN_CHUNK
    n_groups = seq // SUBLANES

    logits = lbl_ref[...]
    e = jnp.exp(logits - jnp.max(logits, axis=1, keepdims=True))
    s_lb = e / jnp.sum(e, axis=1, keepdims=True)
    lb = jnp.zeros((2, LANES), F32)
    for r in range(1, layer + 1):
        lb = lb + s_lb[:, r, :]
    sp = _softplus(-lam_ref[...])

    zeros8 = jnp.zeros((SUBLANES, LANES), F32)
    xpad_ref[pl.ds(0, SUBLANES), :] = zeros8
    xpad_ref[pl.ds(seq + SUBLANES, SUBLANES), :] = zeros8
    xpad_ref[pl.ds(SUBLANES, seq), :] = xa_ref[...]

    def vector_phase(c, carry):
        r0 = pl.multiple_of(c * MIX_ROWS, MIX_ROWS)
        rows = pl.ds(r0, MIX_ROWS)
        row = lax.broadcasted_iota(jnp.int32, (MIX_ROWS, MIX_ROWS), 0)
        col = lax.broadcasted_iota(jnp.int32, (MIX_ROWS, MIX_ROWS), 1)
        same_chunk = (row // HGRN_CHUNK) == (col // HGRN_CHUNK)
        win = xpad_ref[pl.ds(r0, MIX_ROWS + 2 * SUBLANES), :]
        wrows = MIX_ROWS + 2 * SUBLANES
        xc = jnp.zeros((MIX_ROWS, LANES), F32) + cb_ref[...]
        for k in range(CONV_WIDTH):
            shift = (2 - k) % wrows
            shifted = (pltpu.roll(win, shift, 0) if shift else win)[SUBLANES:SUBLANES + MIX_ROWS]
            xc = xc + cw_ref[pl.ds(k, 1), :] * shifted
        gates = _dot(xc.astype(BF16), wg_ref[0]) + bg_ref[0]
        for d in range(2):
            r = jax.nn.sigmoid(gates[:, (2 * d) * LANES:(2 * d + 1) * LANES])
            i = jax.nn.sigmoid(gates[:, (2 * d + 1) * LANES:(2 * d + 2) * LANES])
            log_a = (-RGLRU_C) * r * sp[d:d + 1, :]
            a = jnp.exp(log_a)
            u = jnp.sqrt(-jnp.tanh(log_a) * (1.0 + a * a)) * (i * xc)
            pa, hl = _seg_linear_scan(a, u, SUBLANES, reverse=(d == 1))
            p_ref[d, rows, :] = pa
            l_ref[d, rows, :] = hl
        qv = q_ref[rows, :]
        vv = iv_ref[rows, :].astype(BF16)
        v_ref[rows, :] = vv
        o_acc = jnp.zeros((MIX_ROWS, LANES), F32)
        for d, z_ref in enumerate((zf_ref, zb_ref)):
            sig = jax.nn.sigmoid(z_ref[rows, :])
            lbd = lb[d:d + 1, :]
            log_f = jnp.log(lbd + (1.0 - lbd) * sig)
            kk = (1.0 - lbd) * (1.0 - sig)
            bcum = _seg_cumsum(log_f, HGRN_CHUNK, reverse=(d == 1))
            rest = _seg_cumsum(log_f, HGRN_CHUNK, reverse=(d == 0)) - log_f
            tot_ref[d, rows, :] = bcum + rest
            qe = (qv * jnp.exp(bcum)).astype(BF16)
            ke = (kk * jnp.exp(-bcum)).astype(BF16)
            kd = (kk * jnp.exp(rest)).astype(BF16)
            qe_ref[d, rows, :] = qe
            kd_ref[d, rows, :] = kd
            causal = same_chunk & ((col >= row) if d == 1 else (col <= row))
            att = jnp.where(causal, _dot_nt(qe, ke), 0.0)
            o_acc = o_acc + _dot(att.astype(BF16), vv)
        o_ref[rows, :] = o_acc
        return carry

    lax.fori_loop(0, n_blocks, vector_phase, 0)

    st_ref[...] = jnp.zeros_like(st_ref)

    def chunk_phase(n, carry):
        for d in range(2):
            cidx = n if d == 0 else n_chunks - 1 - n
            r0 = pl.multiple_of(cidx * HGRN_CHUNK, HGRN_CHUNK)
            rows = pl.ds(r0, HGRN_CHUNK)
            st = st_ref[d]
            o_ref[rows, :] += _dot_nt(qe_ref[d, rows, :], st.astype(BF16))
            dec = jnp.exp(tot_ref[d, pl.ds(r0, 1), :])
            st_ref[d] = dec * st + _dot_tn(v_ref[rows, :], kd_ref[d, rows, :])
        return carry

    lax.fori_loop(0, n_chunks, chunk_phase, 0)

    def carry_phase(j, carry):
        cf, cbk = carry
        rf = pl.ds(pl.multiple_of(j * SUBLANES, SUBLANES), SUBLANES)
        hf = p_ref[0, rf, :] * cf + l_ref[0, rf, :]
        l_ref[0, rf, :] = hf
        rb = pl.ds(pl.multiple_of((n_groups - 1 - j) * SUBLANES, SUBLANES), SUBLANES)
        hb = p_ref[1, rb, :] * cbk + l_ref[1, rb, :]
        l_ref[1, rb, :] = hb
        return (jnp.broadcast_to(hf[SUBLANES - 1:SUBLANES, :], (SUBLANES, LANES)),
                jnp.broadcast_to(hb[0:1, :], (SUBLANES, LANES)))

    lax.fori_loop(0, n_groups, carry_phase, (zeros8, zeros8))

    def out_phase(c, carry):
        rows = pl.ds(pl.multiple_of(c * MIX_ROWS, MIX_ROWS), MIX_ROWS)
        y_a = _gelu_tanh(ga_ref[rows, :]) * (l_ref[0, rows, :] + l_ref[1, rows, :])
        gv = g_ref[rows, :]
        y_b = _rms(o_ref[rows, :], ng_ref[...]) * (gv * jax.nn.sigmoid(gv))
        y_ref[rows, pl.ds(0, LANES)] = y_a.astype(y_ref.dtype)
        y_ref[rows, pl.ds(LANES, LANES)] = y_b.astype(y_ref.dtype)
        return carry

    lax.fori_loop(0, n_blocks, out_phase, 0)


def _rec_mixer(layer, proj, batch, seq, conv_w, conv_b, w_gate, b_gate, lam, lb_logits, hg_norm):
    nh = A_HEADS
    col = lambda off: pl.BlockSpec((seq, LANES), lambda b, h, off=off: (b, off + h))
    in_specs = [col(0), col(nh), col(2 * nh), col(3 * nh), col(4 * nh), col(5 * nh), col(6 * nh),
                pl.BlockSpec((CONV_WIDTH, LANES), lambda b, h: (0, h)),
                pl.BlockSpec((1, LANES), lambda b, h: (0, h)),
                pl.BlockSpec((1, LANES, 4 * LANES), lambda b, h: (h, 0, 0)),
                pl.BlockSpec((1, 1, 4 * LANES), lambda b, h: (h, 0, 0)),
                pl.BlockSpec((2, LANES), lambda b, h: (0, h)),
                pl.BlockSpec((2, N_REC_LAYERS, LANES), lambda b, h: (0, 0, h)),
                pl.BlockSpec((1, LANES), lambda b, h: (0, h))]
    return pl.pallas_call(
        functools.partial(_rec_mixer_kernel, layer, seq),
        grid=(batch, nh),
        in_specs=in_specs,
        out_specs=pl.BlockSpec((seq, 2 * LANES), lambda b, h: (b, h)),
        out_shape=jax.ShapeDtypeStruct((batch * seq, D_A + D_B), BF16),
        scratch_shapes=[
            pltpu.VMEM((seq + 2 * SUBLANES, LANES), F32),
            pltpu.VMEM((2, seq, LANES), F32),
            pltpu.VMEM((2, seq, LANES), F32),
            pltpu.VMEM((2, seq, LANES), BF16),
            pltpu.VMEM((2, seq, LANES), BF16),
            pltpu.VMEM((seq, LANES), BF16),
            pltpu.VMEM((seq, LANES), F32),
            pltpu.VMEM((2, seq, LANES), F32),
            pltpu.VMEM((2, LANES, LANES), F32),
        ],
        compiler_params=pltpu.CompilerParams(
            dimension_semantics=("parallel", "parallel"), vmem_limit_bytes=VMEM_LIMIT),
        name="rec_mixer",
    )(proj, proj, proj, proj, proj, proj, proj,
      conv_w, conv_b.reshape(1, D_A), w_gate, b_gate, lam, lb_logits, hg_norm.reshape(1, D_B))


def _rope(t, cos, sin):
    w = t.shape[1]
    lane = lax.broadcasted_iota(jnp.int32, t.shape, 1)
    rot = jnp.where(lane % HEAD_DIM < HEAD_DIM // 2,
                    pltpu.roll(t, w - HEAD_DIM // 2, 1), pltpu.roll(t, HEAD_DIM // 2, 1))
    return t * cos + rot * sin


def _attn_kernel(seq, q_ref, k_ref, v_ref, cos_ref, sin_ref, sink_ref, o_ref, ks_ref, vs_ref):
    n = pl.program_id(1)
    kv_w = N_KV_HEADS * HEAD_DIM
    span = 3 * QBLOCK

    @pl.when(n == 0)
    def _():
        cos_k = jnp.concatenate([cos_ref[...]] * (kv_w // LANES), axis=1)
        sin_k = jnp.concatenate([sin_ref[...]] * (kv_w // LANES), axis=1)
        kr = _rope(k_ref[...], cos_k, sin_k)
        vr = v_ref[...]
        for kvh in range(N_KV_HEADS):
            ks_ref[kvh] = kr[:, kvh * HEAD_DIM:(kvh + 1) * HEAD_DIM].astype(BF16)
            vs_ref[kvh] = vr[:, kvh * HEAD_DIM:(kvh + 1) * HEAD_DIM].astype(BF16)

    q0 = pl.multiple_of(n * QBLOCK, QBLOCK)
    start = pl.multiple_of(jnp.clip(q0 - QBLOCK, 0, seq - span), QBLOCK)
    cos_q = cos_ref[pl.ds(q0, QBLOCK), :]
    sin_q = sin_ref[pl.ds(q0, QBLOCK), :]
    qpos = q0 + lax.broadcasted_iota(jnp.int32, (QBLOCK, span), 0)
    kpos = start + lax.broadcasted_iota(jnp.int32, (QBLOCK, span), 1)
    visible = jnp.abs(qpos - kpos) <= WINDOW
    scale = HEAD_DIM ** -0.5
    for lt in range(N_Q_HEADS * HEAD_DIM // LANES):
        qt = _rope(q_ref[:, pl.ds(lt * LANES, LANES)], cos_q, sin_q) * scale
        outs = []
        for hh in range(LANES // HEAD_DIM):
            head = lt * (LANES // HEAD_DIM) + hh
            kvh = head // GROUP
            qh = qt[:, hh * HEAD_DIM:(hh + 1) * HEAD_DIM].astype(BF16)
            kw = ks_ref[kvh, pl.ds(start, span), :]
            vw = vs_ref[kvh, pl.ds(start, span), :]
            sc = jnp.where(visible, _dot_nt(qh, kw), -jnp.inf)
            sink = sink_ref[head]
            m = jnp.maximum(jnp.max(sc, axis=-1, keepdims=True), sink)
            p = jnp.exp(sc - m)
            denom = jnp.sum(p, axis=-1, keepdims=True) + jnp.exp(sink - m)
            outs.append(_dot(p.astype(BF16), vw) / denom)
        o_ref[:, pl.ds(lt * LANES, LANES)] = jnp.concatenate(outs, axis=1).astype(o_ref.dtype)


def _attention(qkv, batch, seq, cos_t, sin_t, sinks):
    nb = seq // QBLOCK
    q_w = N_Q_HEADS * HEAD_DIM
    kv_w = N_KV_HEADS * HEAD_DIM
    return pl.pallas_call(
        functools.partial(_attn_kernel, seq),
        grid=(batch, nb),
        in_specs=[
            pl.BlockSpec((QBLOCK, q_w), lambda b, n: (b * nb + n, 0)),
            pl.BlockSpec((seq, kv_w), lambda b, n: (b, q_w // kv_w)),
            pl.BlockSpec((seq, kv_w), lambda b, n: (b, q_w // kv_w + 1)),
            pl.BlockSpec((seq, LANES), lambda b, n: (0, 0)),
            pl.BlockSpec((seq, LANES), lambda b, n: (0, 0)),
            pl.BlockSpec(memory_space=pltpu.SMEM),
        ],
        out_specs=pl.BlockSpec((QBLOCK, q_w), lambda b, n: (b * nb + n, 0)),
        out_shape=jax.ShapeDtypeStruct((batch * seq, q_w), BF16),
        scratch_shapes=[pltpu.VMEM((N_KV_HEADS, seq, HEAD_DIM), BF16),
                        pltpu.VMEM((N_KV_HEADS, seq, HEAD_DIM), BF16)],
        compiler_params=pltpu.CompilerParams(
            dimension_semantics=("parallel", "arbitrary"), vmem_limit_bytes=VMEM_LIMIT),
        name="window_attention",
    )(qkv, qkv, qkv, cos_t, sin_t, sinks)


def _rope_tables(seq):
    pos = jnp.arange(seq, dtype=F32)
    inv_freq = ROPE_THETA ** (-jnp.arange(0, HEAD_DIM, 2, dtype=F32) / HEAD_DIM)
    ang = pos[:, None] * inv_freq[None, :]
    cos, sin = jnp.cos(ang), jnp.sin(ang)
    reps = LANES // HEAD_DIM
    return (jnp.tile(jnp.concatenate([cos, cos], axis=1), (1, reps)),
            jnp.tile(jnp.concatenate([-sin, sin], axis=1), (1, reps)))


def kernel(x, norm_g, rec_w_in, rec_conv_w, rec_conv_b, rg_w_r, rg_b_r, rg_w_i, rg_b_i, rg_lambda,
           hgrn_lb_logits, hgrn_norm_g, rec_w_out, att_w_qkv, att_sinks, att_w_o, mlp_w1, mlp_w2):
    batch, seq, d = x.shape
    cos_t, sin_t = _rope_tables(seq)
    h = x.reshape(batch * seq, d)
    for layer in range(DEPTH):
        g = norm_g[layer]
        if layer % 2 == 0:
            r = layer // 2
            proj = _norm_matmul(h, g[0], rec_w_in[r].astype(BF16), 512, F32)
            w_gate = jnp.concatenate([rg_w_r[r, 0], rg_w_i[r, 0], rg_w_r[r, 1], rg_w_i[r, 1]], axis=-1).astype(BF16)
            b_gate = jnp.stack([rg_b_r[r, 0], rg_b_i[r, 0], rg_b_r[r, 1], rg_b_i[r, 1]], axis=0)
            b_gate = b_gate.reshape(4, A_HEADS, A_BLOCK).transpose(1, 0, 2).reshape(A_HEADS, 1, 4 * A_BLOCK)
            y = _rec_mixer(r, proj, batch, seq, rec_conv_w[r], rec_conv_b[r], w_gate, b_gate, rg_lambda[r],
                           hgrn_lb_logits, hgrn_norm_g[r])
            w_out = rec_w_out[r].reshape(2, A_HEADS, A_BLOCK, d).transpose(1, 0, 2, 3).reshape(D_A + D_B, d)
            h = _matmul_norm_res(y, w_out.astype(BF16), g[1], h)
        else:
            a = layer // 2
            qkv = _norm_matmul(h, g[0], att_w_qkv[a].astype(BF16), 512, F32)
            o = _attention(qkv, batch, seq, cos_t, sin_t, att_sinks[a])
            h = _matmul_norm_res(o, att_w_o[a].astype(BF16), g[1], h)
        h = _mlp(h, g[2], mlp_w1[layer].astype(BF16), mlp_w2[layer].astype(BF16), g[3])
    return h.reshape(batch, seq, d)
```

```python
import functools

import jax
import jax.numpy as jnp
from jax import lax
from jax.experimental import pallas as pl
from jax.experimental.pallas import tpu as pltpu

D_MODEL = 1024
DEPTH = 4
N_REC_LAYERS = (DEPTH + 1) // 2
D_A = D_MODEL // 2
A_HEADS = 4
A_BLOCK = D_A // A_HEADS
CONV_WIDTH = 4
RGLRU_C = 8.0
D_B = D_MODEL // 2
B_HEADS = 4
B_DK = D_B // B_HEADS
HGRN_CHUNK = 32
REC_IN = 2 * D_A + 5 * D_B
HEAD_DIM = 64
N_Q_HEADS = D_MODEL // HEAD_DIM
N_KV_HEADS = 4
GROUP = N_Q_HEADS // N_KV_HEADS
WINDOW = 128
QBLOCK = 128
ROPE_THETA = 10000.0
QKV_OUT = (N_Q_HEADS + 2 * N_KV_HEADS) * HEAD_DIM
D_FF = 4 * D_MODEL
EPS = 1e-6

LANES = 128
SUBLANES = 8
ROW_TILE = 1024
FF_TILE = 1024
MIX_ROWS = 256
CHUNK_UNROLL = 8
VMEM_LIMIT = 52 * 1024 * 1024

BF16 = jnp.bfloat16
F32 = jnp.float32


def _rms(x, g):
    return x * lax.rsqrt(jnp.mean(x * x, axis=-1, keepdims=True) + EPS) * g


def _dot(a, b):
    return jnp.dot(a, b, preferred_element_type=F32)


def _dot_nt(a, b):
    return lax.dot_general(a, b, (((1,), (1,)), ((), ())), preferred_element_type=F32)


def _dot_tn(a, b):
    return lax.dot_general(a, b, (((0,), (0,)), ((), ())), preferred_element_type=F32)


def _norm_matmul_kernel(x_ref, g_ref, w_ref, o_ref, xn_ref):
    @pl.when(pl.program_id(1) == 0)
    def _():
        xn_ref[...] = _rms(x_ref[...], g_ref[...]).astype(BF16)

    o_ref[...] = _dot(xn_ref[...], w_ref[...]).astype(o_ref.dtype)


def _norm_matmul(x, g, w, col_tile, out_dtype):
    m, k = x.shape
    n = w.shape[1]
    return pl.pallas_call(
        _norm_matmul_kernel,
        grid=(m // ROW_TILE, n // col_tile),
        in_specs=[
            pl.BlockSpec((ROW_TILE, k), lambda i, j: (i, 0)),
            pl.BlockSpec((1, k), lambda i, j: (0, 0)),
            pl.BlockSpec((k, col_tile), lambda i, j: (0, j)),
        ],
        out_specs=pl.BlockSpec((ROW_TILE, col_tile), lambda i, j: (i, j)),
        out_shape=jax.ShapeDtypeStruct((m, n), out_dtype),
        scratch_shapes=[pltpu.VMEM((ROW_TILE, k), BF16)],
        compiler_params=pltpu.CompilerParams(
            dimension_semantics=("parallel", "arbitrary"), vmem_limit_bytes=VMEM_LIMIT),
        name="norm_matmul",
    )(x, g.reshape(1, k), w)


def _matmul_norm_res_kernel(a_ref, w_ref, g_ref, h_ref, o_ref):
    m = _dot(a_ref[...], w_ref[...])
    o_ref[...] = h_ref[...] + _rms(m, g_ref[...])


def _matmul_norm_res(a, w, g, h):
    m, k = a.shape
    n = w.shape[1]
    return pl.pallas_call(
        _matmul_norm_res_kernel,
        grid=(m // ROW_TILE,),
        in_specs=[
            pl.BlockSpec((ROW_TILE, k), lambda i: (i, 0)),
            pl.BlockSpec((k, n), lambda i: (0, 0)),
            pl.BlockSpec((1, n), lambda i: (0, 0)),
            pl.BlockSpec((ROW_TILE, n), lambda i: (i, 0)),
        ],
        out_specs=pl.BlockSpec((ROW_TILE, n), lambda i: (i, 0)),
        out_shape=jax.ShapeDtypeStruct((m, n), F32),
        compiler_params=pltpu.CompilerParams(
            dimension_semantics=("parallel",), vmem_limit_bytes=VMEM_LIMIT),
        name="matmul_norm_res",
    )(a, w, g.reshape(1, n), h)


def _mlp_kernel(h_ref, g_in_ref, w1_ref, w2_ref, g_out_ref, o_ref, xn_ref, acc_ref):
    j = pl.program_id(1)

    @pl.when(j == 0)
    def _():
        xn_ref[...] = _rms(h_ref[...], g_in_ref[...]).astype(BF16)
        acc_ref[...] = jnp.zeros_like(acc_ref)

    hid = jnp.maximum(_dot(xn_ref[...], w1_ref[...]), 0.0)
    acc_ref[...] += _dot((hid * hid).astype(BF16), w2_ref[...])

    @pl.when(j == pl.num_programs(1) - 1)
    def _():
        o_ref[...] = h_ref[...] + _rms(acc_ref[...], g_out_ref[...])


def _mlp(h, g_in, w1, w2, g_out):
    m, d = h.shape
    ff = w1.shape[1]
    return pl.pallas_call(
        _mlp_kernel,
        grid=(m // ROW_TILE, ff // FF_TILE),
        in_specs=[
            pl.BlockSpec((ROW_TILE, d), lambda i, j: (i, 0)),
            pl.BlockSpec((1, d), lambda i, j: (0, 0)),
            pl.BlockSpec((d, FF_TILE), lambda i, j: (0, j)),
            pl.BlockSpec((FF_TILE, d), lambda i, j: (j, 0)),
            pl.BlockSpec((1, d), lambda i, j: (0, 0)),
        ],
        out_specs=pl.BlockSpec((ROW_TILE, d), lambda i, j: (i, 0)),
        out_shape=jax.ShapeDtypeStruct((m, d), F32),
        scratch_shapes=[pltpu.VMEM((ROW_TILE, d), BF16), pltpu.VMEM((ROW_TILE, d), F32)],
        compiler_params=pltpu.CompilerParams(
            dimension_semantics=("parallel", "arbitrary"), vmem_limit_bytes=VMEM_LIMIT),
        name="mlp",
    )(h, g_in.reshape(1, d), w1, w2, g_out.reshape(1, d))


def _row_ids(rows):
    return lax.broadcasted_iota(jnp.int32, (rows, LANES), 0)


def _shift_down(x, k):
    return pltpu.roll(x, k, 0)


def _shift_up(x, k):
    return pltpu.roll(x, x.shape[0] - k, 0)


def _chunk_sum_matrix():
    i = jnp.arange(MIX_ROWS)[:, None]
    j = jnp.arange(MIX_ROWS)[None, :]
    same = (i // HGRN_CHUNK) == (j // HGRN_CHUNK)
    return jnp.concatenate([same & (j <= i), same & (j > i)], axis=0).astype(BF16)


def _seg_linear_scan(a, u, seg, reverse):
    pos = _row_ids(a.shape[0]) % seg
    k = 1
    while k < seg:
        if reverse:
            keep = pos < seg - k
            a_n, u_n = _shift_up(a, k), _shift_up(u, k)
        else:
            keep = pos >= k
            a_n, u_n = _shift_down(a, k), _shift_down(u, k)
        u = u + jnp.where(keep, a * u_n, 0.0)
        a = a * jnp.where(keep, a_n, 1.0)
        k *= 2
    return a, u


def _softplus(x):
    return jnp.maximum(x, 0.0) + jnp.log1p(jnp.exp(-jnp.abs(x)))


def _gelu_tanh(x):
    return 0.5 * x * (1.0 + jnp.tanh(0.7978845608028654 * (x + 0.044715 * (x * x * x))))


def _rec_mixer_kernel(layer, seq,
                      xa_ref, ga_ref, q_ref, zf_ref, zb_ref, iv_ref, g_ref,
                      cw_ref, cb_ref, wg_ref, bg_ref, lam_ref, lbl_ref, ng_ref, tri_ref,
                      y_ref,
                      xpad_ref, p_ref, l_ref, qe_ref, kd_ref, v_ref, o_ref, tot_ref, st_ref):
    n_blocks = seq // MIX_ROWS
    n_chunks = seq // HGRN_CHUNK
    n_groups = seq // SUBLANES

    logits = lbl_ref[...]
    e = jnp.exp(logits - jnp.max(logits, axis=1, keepdims=True))
    s_lb = e / jnp.sum(e, axis=1, keepdims=True)
    lb = jnp.zeros((2, LANES), F32)
    for r in range(1, layer + 1):
        lb = lb + s_lb[:, r, :]
    sp = _softplus(-lam_ref[...])

    zeros8 = jnp.zeros((SUBLANES, LANES), F32)
    xpad_ref[pl.ds(0, SUBLANES), :] = zeros8
    xpad_ref[pl.ds(seq + SUBLANES, SUBLANES), :] = zeros8
    xpad_ref[pl.ds(SUBLANES, seq), :] = xa_ref[...]

    def vector_phase(c, carry):
        r0 = pl.multiple_of(c * MIX_ROWS, MIX_ROWS)
        rows = pl.ds(r0, MIX_ROWS)
        row = lax.broadcasted_iota(jnp.int32, (MIX_ROWS, MIX_ROWS), 0)
        col = lax.broadcasted_iota(jnp.int32, (MIX_ROWS, MIX_ROWS), 1)
        same_chunk = (row // HGRN_CHUNK) == (col // HGRN_CHUNK)
        win = xpad_ref[pl.ds(r0, MIX_ROWS + 2 * SUBLANES), :]
        wrows = MIX_ROWS + 2 * SUBLANES
        xc = jnp.zeros((MIX_ROWS, LANES), F32) + cb_ref[...]
        for k in range(CONV_WIDTH):
            shift = (2 - k) % wrows
            shifted = (pltpu.roll(win, shift, 0) if shift else win)[SUBLANES:SUBLANES + MIX_ROWS]
            xc = xc + cw_ref[pl.ds(k, 1), :] * shifted
        gates = _dot(xc.astype(BF16), wg_ref[0]) + bg_ref[0]
        for d in range(2):
            r = jax.nn.sigmoid(gates[:, (2 * d) * LANES:(2 * d + 1) * LANES])
            i = jax.nn.sigmoid(gates[:, (2 * d + 1) * LANES:(2 * d + 2) * LANES])
            log_a = (-RGLRU_C) * r * sp[d:d + 1, :]
            a = jnp.exp(log_a)
            u = jnp.sqrt(-jnp.tanh(log_a) * (1.0 + a * a)) * (i * xc)
            pa, hl = _seg_linear_scan(a, u, SUBLANES, reverse=(d == 1))
            p_ref[d, rows, :] = pa
            l_ref[d, rows, :] = hl
        qv = q_ref[rows, :]
        vv = iv_ref[rows, :].astype(BF16)
        v_ref[rows, :] = vv
        o_acc = jnp.zeros((MIX_ROWS, LANES), F32)
        log_fs, kks, parts = [], [], []
        for d, z_ref in enumerate((zf_ref, zb_ref)):
            sig = jax.nn.sigmoid(z_ref[rows, :])
            lbd = lb[d:d + 1, :]
            log_f = jnp.log(lbd + (1.0 - lbd) * sig)
            log_fs.append(log_f)
            kks.append((1.0 - lbd) * (1.0 - sig))
            hi = log_f.astype(BF16)
            parts += [hi, (log_f - hi.astype(F32)).astype(BF16)]
        sums = _dot(tri_ref[...], jnp.concatenate(parts, axis=1))
        pre_f = sums[:MIX_ROWS, 0:LANES] + sums[:MIX_ROWS, LANES:2 * LANES]
        suf_f = sums[MIX_ROWS:, 0:LANES] + sums[MIX_ROWS:, LANES:2 * LANES]
        pre_b = sums[:MIX_ROWS, 2 * LANES:3 * LANES] + sums[:MIX_ROWS, 3 * LANES:]
        suf_b = sums[MIX_ROWS:, 2 * LANES:3 * LANES] + sums[MIX_ROWS:, 3 * LANES:]
        bcums = (pre_f, suf_b + log_fs[1])
        rests = (suf_f, pre_b - log_fs[1])
        for d in range(2):
            bcum, rest, kk = bcums[d], rests[d], kks[d]
            tot_ref[d, rows, :] = bcum + rest
            qe = (qv * jnp.exp(bcum)).astype(BF16)
            ke = (kk * jnp.exp(-bcum)).astype(BF16)
            kd = (kk * jnp.exp(rest)).astype(BF16)
            qe_ref[d, rows, :] = qe
            kd_ref[d, rows, :] = kd
            causal = same_chunk & ((col >= row) if d == 1 else (col <= row))
            att = jnp.where(causal, _dot_nt(qe, ke), 0.0)
            o_acc = o_acc + _dot(att.astype(BF16), vv)
        o_ref[rows, :] = o_acc
        return carry

    lax.fori_loop(0, n_blocks, vector_phase, 0)

    st_ref[...] = jnp.zeros_like(st_ref)

    def chunk_phase(n, carry):
        for d in range(2):
            cidx = n if d == 0 else n_chunks - 1 - n
            r0 = pl.multiple_of(cidx * HGRN_CHUNK, HGRN_CHUNK)
            rows = pl.ds(r0, HGRN_CHUNK)
            st = st_ref[d]
            o_ref[rows, :] += _dot_nt(qe_ref[d, rows, :], st.astype(BF16))
            dec = jnp.exp(tot_ref[d, pl.ds(r0, 1), :])
            st_ref[d] = dec * st + _dot_tn(v_ref[rows, :], kd_ref[d, rows, :])
        return carry

    lax.fori_loop(0, n_chunks, chunk_phase, 0, unroll=CHUNK_UNROLL)

    def carry_phase(j, carry):
        cf, cbk = carry
        rf = pl.ds(pl.multiple_of(j * SUBLANES, SUBLANES), SUBLANES)
        hf = p_ref[0, rf, :] * cf + l_ref[0, rf, :]
        l_ref[0, rf, :] = hf
        rb = pl.ds(pl.multiple_of((n_groups - 1 - j) * SUBLANES, SUBLANES), SUBLANES)
        hb = p_ref[1, rb, :] * cbk + l_ref[1, rb, :]
        l_ref[1, rb, :] = hb
        return (jnp.broadcast_to(hf[SUBLANES - 1:SUBLANES, :], (SUBLANES, LANES)),
                jnp.broadcast_to(hb[0:1, :], (SUBLANES, LANES)))

    lax.fori_loop(0, n_groups, carry_phase, (zeros8, zeros8))

    def out_phase(c, carry):
        rows = pl.ds(pl.multiple_of(c * MIX_ROWS, MIX_ROWS), MIX_ROWS)
        y_a = _gelu_tanh(ga_ref[rows, :]) * (l_ref[0, rows, :] + l_ref[1, rows, :])
        gv = g_ref[rows, :]
        y_b = _rms(o_ref[rows, :], ng_ref[...]) * (gv * jax.nn.sigmoid(gv))
        y_ref[rows, pl.ds(0, LANES)] = y_a.astype(y_ref.dtype)
        y_ref[rows, pl.ds(LANES, LANES)] = y_b.astype(y_ref.dtype)
        return carry

    lax.fori_loop(0, n_blocks, out_phase, 0)


def _rec_mixer(layer, proj, batch, seq, conv_w, conv_b, w_gate, b_gate, lam, lb_logits, hg_norm):
    nh = A_HEADS
    col = lambda off: pl.BlockSpec((seq, LANES), lambda b, h, off=off: (b, off + h))
    in_specs = [col(0), col(nh), col(2 * nh), col(3 * nh), col(4 * nh), col(5 * nh), col(6 * nh),
                pl.BlockSpec((CONV_WIDTH, LANES), lambda b, h: (0, h)),
                pl.BlockSpec((1, LANES), lambda b, h: (0, h)),
                pl.BlockSpec((1, LANES, 4 * LANES), lambda b, h: (h, 0, 0)),
                pl.BlockSpec((1, 1, 4 * LANES), lambda b, h: (h, 0, 0)),
                pl.BlockSpec((2, LANES), lambda b, h: (0, h)),
                pl.BlockSpec((2, N_REC_LAYERS, LANES), lambda b, h: (0, 0, h)),
                pl.BlockSpec((1, LANES), lambda b, h: (0, h)),
                pl.BlockSpec((2 * MIX_ROWS, MIX_ROWS), lambda b, h: (0, 0))]
    return pl.pallas_call(
        functools.partial(_rec_mixer_kernel, layer, seq),
        grid=(batch, nh),
        in_specs=in_specs,
        out_specs=pl.BlockSpec((seq, 2 * LANES), lambda b, h: (b, h)),
        out_shape=jax.ShapeDtypeStruct((batch * seq, D_A + D_B), BF16),
        scratch_shapes=[
            pltpu.VMEM((seq + 2 * SUBLANES, LANES), F32),
            pltpu.VMEM((2, seq, LANES), F32),
            pltpu.VMEM((2, seq, LANES), F32),
            pltpu.VMEM((2, seq, LANES), BF16),
            pltpu.VMEM((2, seq, LANES), BF16),
            pltpu.VMEM((seq, LANES), BF16),
            pltpu.VMEM((seq, LANES), F32),
            pltpu.VMEM((2, seq, LANES), F32),
            pltpu.VMEM((2, LANES, LANES), F32),
        ],
        compiler_params=pltpu.CompilerParams(
            dimension_semantics=("parallel", "parallel"), vmem_limit_bytes=VMEM_LIMIT),
        name="rec_mixer",
    )(proj, proj, proj, proj, proj, proj, proj,
      conv_w, conv_b.reshape(1, D_A), w_gate, b_gate, lam, lb_logits, hg_norm.reshape(1, D_B), _chunk_sum_matrix())


def _rope(t, cos, sin):
    w = t.shape[1]
    lane = lax.broadcasted_iota(jnp.int32, t.shape, 1)
    rot = jnp.where(lane % HEAD_DIM < HEAD_DIM // 2,
                    pltpu.roll(t, w - HEAD_DIM // 2, 1), pltpu.roll(t, HEAD_DIM // 2, 1))
    return t * cos + rot * sin


def _attn_kernel(seq, q_ref, k_ref, v_ref, cos_ref, sin_ref, sink_ref, o_ref, ks_ref, vt_ref):
    n = pl.program_id(1)
    span = 3 * QBLOCK
    heads_per_tile = LANES // HEAD_DIM
    tiles_per_group = GROUP // heads_per_tile

    @pl.when(n == 0)
    def _():
        def prep(j, carry):
            r = pl.ds(pl.multiple_of(j * QBLOCK, QBLOCK), QBLOCK)
            cos_b, sin_b = cos_ref[r, :], sin_ref[r, :]
            upper = lax.broadcasted_iota(jnp.int32, (QBLOCK, LANES), 1) >= HEAD_DIM
            for t in range(N_KV_HEADS // heads_per_tile):
                kt = _rope(k_ref[r, pl.ds(t * LANES, LANES)], cos_b, sin_b)
                vt = v_ref[r, pl.ds(t * LANES, LANES)]
                for hh in range(heads_per_tile):
                    g = t * heads_per_tile + hh
                    own = upper if hh else jnp.logical_not(upper)
                    k_same = jnp.where(own, kt, 0.0)
                    v_same = jnp.where(own, vt, 0.0)
                    k_swap = pltpu.roll(k_same, HEAD_DIM, 1)
                    v_swap = pltpu.roll(v_same, HEAD_DIM, 1)
                    k_halves = (k_same, k_swap) if hh == 0 else (k_swap, k_same)
                    v_halves = (v_same, v_swap) if hh == 0 else (v_swap, v_same)
                    for p in range(heads_per_tile):
                        ks_ref[p, g, r, :] = k_halves[p].astype(BF16)
                        vt_ref[p, g, j] = v_halves[p].T.astype(BF16)
            return carry

        lax.fori_loop(0, seq // QBLOCK, prep, 0)

    q0 = pl.multiple_of(n * QBLOCK, QBLOCK)
    start = pl.multiple_of(jnp.clip(q0 - QBLOCK, 0, seq - span), QBLOCK)
    first_block = start // QBLOCK
    cos_q = cos_ref[pl.ds(q0, QBLOCK), :]
    sin_q = sin_ref[pl.ds(q0, QBLOCK), :]
    kpos = start + lax.broadcasted_iota(jnp.int32, (span, LANES), 0)
    qpos = q0 + lax.broadcasted_iota(jnp.int32, (span, LANES), 1)
    bias = jnp.where(jnp.abs(qpos - kpos) <= WINDOW, 0.0, -jnp.inf)
    bias = jnp.concatenate([bias] * tiles_per_group, axis=1)
    second_tile = lax.broadcasted_iota(jnp.int32, (1, tiles_per_group * LANES), 1) >= LANES
    scale = HEAD_DIM ** -0.5
    for g in range(N_KV_HEADS):
        qs = jnp.concatenate(
            [(_rope(q_ref[:, pl.ds((g * tiles_per_group + t) * LANES, LANES)], cos_q, sin_q) * scale).astype(BF16)
             for t in range(tiles_per_group)], axis=0)
        acc = jnp.zeros((LANES, tiles_per_group * QBLOCK), F32)
        for p in range(heads_per_tile):
            sc = _dot_nt(ks_ref[p, g, pl.ds(start, span), :], qs) + bias
            sink = jnp.where(second_tile, sink_ref[g * GROUP + heads_per_tile + p], sink_ref[g * GROUP + p])
            m = jnp.maximum(jnp.max(sc, axis=0, keepdims=True), sink)
            e = jnp.exp(sc - m)
            denom = jnp.sum(e, axis=0, keepdims=True) + jnp.exp(sink - m)
            pt = (e * (1.0 / denom)).astype(BF16)
            for j in range(span // QBLOCK):
                acc = acc + _dot(vt_ref[p, g, first_block + j], pt[j * QBLOCK:(j + 1) * QBLOCK, :])
        for t in range(tiles_per_group):
            o_ref[:, pl.ds((g * tiles_per_group + t) * LANES, LANES)] = (
                acc[:, t * QBLOCK:(t + 1) * QBLOCK].T.astype(o_ref.dtype))


def _attention(qkv, batch, seq, cos_t, sin_t, sinks):
    nb = seq // QBLOCK
    q_w = N_Q_HEADS * HEAD_DIM
    kv_w = N_KV_HEADS * HEAD_DIM
    return pl.pallas_call(
        functools.partial(_attn_kernel, seq),
        grid=(batch, nb),
        in_specs=[
            pl.BlockSpec((QBLOCK, q_w), lambda b, n: (b * nb + n, 0)),
            pl.BlockSpec((seq, kv_w), lambda b, n: (b, q_w // kv_w)),
            pl.BlockSpec((seq, kv_w), lambda b, n: (b, q_w // kv_w + 1)),
            pl.BlockSpec((seq, LANES), lambda b, n: (0, 0)),
            pl.BlockSpec((seq, LANES), lambda b, n: (0, 0)),
            pl.BlockSpec(memory_space=pltpu.SMEM),
        ],
        out_specs=pl.BlockSpec((QBLOCK, q_w), lambda b, n: (b * nb + n, 0)),
        out_shape=jax.ShapeDtypeStruct((batch * seq, q_w), BF16),
        scratch_shapes=[pltpu.VMEM((LANES // HEAD_DIM, N_KV_HEADS, seq, LANES), BF16),
                        pltpu.VMEM((LANES // HEAD_DIM, N_KV_HEADS, nb, LANES, QBLOCK), BF16)],
        compiler_params=pltpu.CompilerParams(
            dimension_semantics=("parallel", "arbitrary"), vmem_limit_bytes=VMEM_LIMIT),
        name="window_attention",
    )(qkv, qkv, qkv, cos_t, sin_t, sinks)


def _rope_tables(seq):
    pos = jnp.arange(seq, dtype=F32)
    inv_freq = ROPE_THETA ** (-jnp.arange(0, HEAD_DIM, 2, dtype=F32) / HEAD_DIM)
    ang = pos[:, None] * inv_freq[None, :]
    cos, sin = jnp.cos(ang), jnp.sin(ang)
    reps = LANES // HEAD_DIM
    return (jnp.tile(jnp.concatenate([cos, cos], axis=1), (1, reps)),
            jnp.tile(jnp.concatenate([-sin, sin], axis=1), (1, reps)))


def kernel(x, norm_g, rec_w_in, rec_conv_w, rec_conv_b, rg_w_r, rg_b_r, rg_w_i, rg_b_i, rg_lambda,
           hgrn_lb_logits, hgrn_norm_g, rec_w_out, att_w_qkv, att_sinks, att_w_o, mlp_w1, mlp_w2):
    batch, seq, d = x.shape
    cos_t, sin_t = _rope_tables(seq)
    h = x.reshape(batch * seq, d)
    for layer in range(DEPTH):
        g = norm_g[layer]
        if layer % 2 == 0:
            r = layer // 2
            proj = _norm_matmul(h, g[0], rec_w_in[r].astype(BF16), 512, F32)
            w_gate = jnp.concatenate([rg_w_r[r, 0], rg_w_i[r, 0], rg_w_r[r, 1], rg_w_i[r, 1]], axis=-1).astype(BF16)
            b_gate = jnp.stack([rg_b_r[r, 0], rg_b_i[r, 0], rg_b_r[r, 1], rg_b_i[r, 1]], axis=0)
            b_gate = b_gate.reshape(4, A_HEADS, A_BLOCK).transpose(1, 0, 2).reshape(A_HEADS, 1, 4 * A_BLOCK)
            y = _rec_mixer(r, proj, batch, seq, rec_conv_w[r], rec_conv_b[r], w_gate, b_gate, rg_lambda[r],
                           hgrn_lb_logits, hgrn_norm_g[r])
            w_out = rec_w_out[r].reshape(2, A_HEADS, A_BLOCK, d).transpose(1, 0, 2, 3).reshape(D_A + D_B, d)
            h = _matmul_norm_res(y, w_out.astype(BF16), g[1], h)
        else:
            a = layer // 2
            qkv = _norm_matmul(h, g[0], att_w_qkv[a].astype(BF16), 512, F32)
            o = _attention(qkv, batch, seq, cos_t, sin_t, att_sinks[a])
            h = _matmul_norm_res(o, att_w_o[a].astype(BF16), g[1], h)
        h = _mlp(h, g[2], mlp_w1[layer].astype(BF16), mlp_w2[layer].astype(BF16), g[3])
    return h.reshape(batch, seq, d)
```

```python
import functools

import jax
import jax.numpy as jnp
from jax import lax
from jax.experimental import pallas as pl
from jax.experimental.pallas import tpu as pltpu

D_MODEL = 1024
DEPTH = 4
N_REC_LAYERS = (DEPTH + 1) // 2
D_A = D_MODEL // 2
A_HEADS = 4
A_BLOCK = D_A // A_HEADS
CONV_WIDTH = 4
RGLRU_C = 8.0
D_B = D_MODEL // 2
B_HEADS = 4
B_DK = D_B // B_HEADS
HGRN_CHUNK = 32
REC_IN = 2 * D_A + 5 * D_B
HEAD_DIM = 64
N_Q_HEADS = D_MODEL // HEAD_DIM
N_KV_HEADS = 4
GROUP = N_Q_HEADS // N_KV_HEADS
WINDOW = 128
QBLOCK = 128
ROPE_THETA = 10000.0
QKV_OUT = (N_Q_HEADS + 2 * N_KV_HEADS) * HEAD_DIM
D_FF = 4 * D_MODEL
EPS = 1e-6

LANES = 128
SUBLANES = 8
ROW_TILE = 1024
FF_TILE = 1024
MIX_ROWS = 256
SEG_PITCH = MIX_ROWS + SUBLANES
SCAN_UNROLL = 8
VMEM_LIMIT = 52 * 1024 * 1024

BF16 = jnp.bfloat16
F32 = jnp.float32


def _rms(x, g):
    return x * lax.rsqrt(jnp.mean(x * x, axis=-1, keepdims=True) + EPS) * g


def _dot(a, b):
    return jnp.dot(a, b, preferred_element_type=F32)


def _dot_nt(a, b):
    return lax.dot_general(a, b, (((1,), (1,)), ((), ())), preferred_element_type=F32)


def _dot_tn(a, b):
    return lax.dot_general(a, b, (((0,), (0,)), ((), ())), preferred_element_type=F32)


def _norm_matmul_kernel(x_ref, g_ref, w_ref, o_ref, xn_ref):
    @pl.when(pl.program_id(1) == 0)
    def _():
        xn_ref[...] = _rms(x_ref[...], g_ref[...]).astype(BF16)

    o_ref[...] = _dot(xn_ref[...], w_ref[...]).astype(o_ref.dtype)


def _norm_matmul(x, g, w, col_tile, out_dtype):
    m, k = x.shape
    n = w.shape[1]
    return pl.pallas_call(
        _norm_matmul_kernel,
        grid=(m // ROW_TILE, n // col_tile),
        in_specs=[
            pl.BlockSpec((ROW_TILE, k), lambda i, j: (i, 0)),
            pl.BlockSpec((1, k), lambda i, j: (0, 0)),
            pl.BlockSpec((k, col_tile), lambda i, j: (0, j)),
        ],
        out_specs=pl.BlockSpec((ROW_TILE, col_tile), lambda i, j: (i, j)),
        out_shape=jax.ShapeDtypeStruct((m, n), out_dtype),
        scratch_shapes=[pltpu.VMEM((ROW_TILE, k), BF16)],
        compiler_params=pltpu.CompilerParams(
            dimension_semantics=("parallel", "arbitrary"), vmem_limit_bytes=VMEM_LIMIT),
        name="norm_matmul",
    )(x, g.reshape(1, k), w)


def _matmul_norm_res_kernel(a_ref, w_ref, g_ref, h_ref, o_ref):
    m = _dot(a_ref[...], w_ref[...])
    o_ref[...] = h_ref[...] + _rms(m, g_ref[...])


def _matmul_norm_res(a, w, g, h):
    m, k = a.shape
    n = w.shape[1]
    return pl.pallas_call(
        _matmul_norm_res_kernel,
        grid=(m // ROW_TILE,),
        in_specs=[
            pl.BlockSpec((ROW_TILE, k), lambda i: (i, 0)),
            pl.BlockSpec((k, n), lambda i: (0, 0)),
            pl.BlockSpec((1, n), lambda i: (0, 0)),
            pl.BlockSpec((ROW_TILE, n), lambda i: (i, 0)),
        ],
        out_specs=pl.BlockSpec((ROW_TILE, n), lambda i: (i, 0)),
        out_shape=jax.ShapeDtypeStruct((m, n), F32),
        compiler_params=pltpu.CompilerParams(
            dimension_semantics=("parallel",), vmem_limit_bytes=VMEM_LIMIT),
        name="matmul_norm_res",
    )(a, w, g.reshape(1, n), h)


def _mlp_kernel(h_ref, g_in_ref, w1_ref, w2_ref, g_out_ref, o_ref, xn_ref, acc_ref):
    j = pl.program_id(1)

    @pl.when(j == 0)
    def _():
        xn_ref[...] = _rms(h_ref[...], g_in_ref[...]).astype(BF16)
        acc_ref[...] = jnp.zeros_like(acc_ref)

    hid = jnp.maximum(_dot(xn_ref[...], w1_ref[...]), 0.0)
    acc_ref[...] += _dot((hid * hid).astype(BF16), w2_ref[...])

    @pl.when(j == pl.num_programs(1) - 1)
    def _():
        o_ref[...] = h_ref[...] + _rms(acc_ref[...], g_out_ref[...])


def _mlp(h, g_in, w1, w2, g_out):
    m, d = h.shape
    ff = w1.shape[1]
    return pl.pallas_call(
        _mlp_kernel,
        grid=(m // ROW_TILE, ff // FF_TILE),
        in_specs=[
            pl.BlockSpec((ROW_TILE, d), lambda i, j: (i, 0)),
            pl.BlockSpec((1, d), lambda i, j: (0, 0)),
            pl.BlockSpec((d, FF_TILE), lambda i, j: (0, j)),
            pl.BlockSpec((FF_TILE, d), lambda i, j: (j, 0)),
            pl.BlockSpec((1, d), lambda i, j: (0, 0)),
        ],
        out_specs=pl.BlockSpec((ROW_TILE, d), lambda i, j: (i, 0)),
        out_shape=jax.ShapeDtypeStruct((m, d), F32),
        scratch_shapes=[pltpu.VMEM((ROW_TILE, d), BF16), pltpu.VMEM((ROW_TILE, d), F32)],
        compiler_params=pltpu.CompilerParams(
            dimension_semantics=("parallel", "arbitrary"), vmem_limit_bytes=VMEM_LIMIT),
        name="mlp",
    )(h, g_in.reshape(1, d), w1, w2, g_out.reshape(1, d))


def _shift_down(x, k):
    return pltpu.roll(x, k, 0)


def _shift_up(x, k):
    return pltpu.roll(x, x.shape[0] - k, 0)


def _chunk_sum_matrix():
    i = jnp.arange(MIX_ROWS)[:, None]
    j = jnp.arange(MIX_ROWS)[None, :]
    same = (i // HGRN_CHUNK) == (j // HGRN_CHUNK)
    return jnp.concatenate([same & (j <= i), same & (j > i)], axis=0).astype(BF16)


def _chunk_causal_masks():
    i = jnp.arange(MIX_ROWS)[:, None]
    j = jnp.arange(MIX_ROWS)[None, :]
    same = (i // HGRN_CHUNK) == (j // HGRN_CHUNK)
    return jnp.stack([same & (j <= i), same & (j >= i)]).astype(F32)


def _sublane_linear_scan(a, u, reverse):
    pos = lax.broadcasted_iota(jnp.int32, a.shape, 0)
    k = 1
    while k < SUBLANES:
        if reverse:
            keep = pos < SUBLANES - k
            a_n, u_n = _shift_up(a, k), _shift_up(u, k)
        else:
            keep = pos >= k
            a_n, u_n = _shift_down(a, k), _shift_down(u, k)
        u = u + jnp.where(keep, a * u_n, 0.0)
        a = a * jnp.where(keep, a_n, 1.0)
        k *= 2
    return u


def _sigmoid(x):
    return 0.5 * jnp.tanh(0.5 * x) + 0.5


def _softplus(x):
    return jnp.maximum(x, 0.0) + jnp.log1p(jnp.exp(-jnp.abs(x)))


def _gelu_tanh(x):
    return 0.5 * x * (1.0 + jnp.tanh(0.7978845608028654 * (x + 0.044715 * (x * x * x))))


def _rec_mixer_kernel(layer, seq,
                      xa_ref, ga_ref, q_ref, zf_ref, zb_ref, iv_ref, g_ref,
                      cw_ref, cb_ref, wg_ref, bg_ref, lam_ref, lbl_ref, ng_ref, tri_ref, mask_ref,
                      y_ref,
                      xpad_ref, a_ref, u_ref, pcum_ref, hloc_ref, cin_ref, o_ref, st_ref):
    n_blocks = seq // MIX_ROWS
    chunks = MIX_ROWS // HGRN_CHUNK
    assert n_blocks == SUBLANES

    logits = lbl_ref[...]
    e = jnp.exp(logits - jnp.max(logits, axis=1, keepdims=True))
    s_lb = e / jnp.sum(e, axis=1, keepdims=True)
    lb = jnp.zeros((2, LANES), F32)
    for r in range(1, layer + 1):
        lb = lb + s_lb[:, r, :]
    sp = _softplus(-lam_ref[...])

    zeros8 = jnp.zeros((SUBLANES, LANES), F32)
    xpad_ref[pl.ds(0, SUBLANES), :] = zeros8
    xpad_ref[pl.ds(seq + SUBLANES, SUBLANES), :] = zeros8
    xpad_ref[pl.ds(SUBLANES, seq), :] = xa_ref[...].astype(F32)
    st_ref[...] = jnp.zeros_like(st_ref)

    def vector_phase(c, carry):
        r0 = pl.multiple_of(c * MIX_ROWS, MIX_ROWS)
        seg_rows = pl.ds(pl.multiple_of(c * SEG_PITCH, SUBLANES), MIX_ROWS)
        win = xpad_ref[pl.ds(r0, MIX_ROWS + 2 * SUBLANES), :]
        wrows = MIX_ROWS + 2 * SUBLANES
        xc = jnp.zeros((MIX_ROWS, LANES), F32) + cb_ref[...]
        for k in range(CONV_WIDTH):
            shift = (2 - k) % wrows
            shifted = (pltpu.roll(win, shift, 0) if shift else win)[SUBLANES:SUBLANES + MIX_ROWS]
            xc = xc + cw_ref[pl.ds(k, 1), :] * shifted
        gates = _dot(xc.astype(BF16), wg_ref[0]) + bg_ref[0]
        for d in range(2):
            r = _sigmoid(gates[:, (2 * d) * LANES:(2 * d + 1) * LANES])
            i = _sigmoid(gates[:, (2 * d + 1) * LANES:(2 * d + 2) * LANES])
            log_a = (-RGLRU_C) * r * sp[d:d + 1, :]
            a = jnp.exp(log_a)
            a_ref[d, seg_rows, :] = a
            u_ref[d, seg_rows, :] = jnp.sqrt(-jnp.tanh(log_a) * (1.0 + a * a)) * (i * xc)

        rows = [pl.ds(pl.multiple_of(blk * MIX_ROWS, MIX_ROWS), MIX_ROWS) for blk in (c, n_blocks - 1 - c)]
        log_fs, kks, parts = [], [], []
        for d, z_ref in enumerate((zf_ref, zb_ref)):
            sig = _sigmoid(z_ref[rows[d], :].astype(F32))
            lbd = lb[d:d + 1, :]
            log_f = jnp.log(lbd + (1.0 - lbd) * sig)
            log_fs.append(log_f)
            kks.append((1.0 - lbd) * (1.0 - sig))
            hi = log_f.astype(BF16)
            parts += [hi, (log_f - hi.astype(F32)).astype(BF16)]
        sums = _dot(tri_ref[...], jnp.concatenate(parts, axis=1))
        pre_f = sums[:MIX_ROWS, 0:LANES] + sums[:MIX_ROWS, LANES:2 * LANES]
        suf_f = sums[MIX_ROWS:, 0:LANES] + sums[MIX_ROWS:, LANES:2 * LANES]
        pre_b = sums[:MIX_ROWS, 2 * LANES:3 * LANES] + sums[:MIX_ROWS, 3 * LANES:]
        suf_b = sums[MIX_ROWS:, 2 * LANES:3 * LANES] + sums[MIX_ROWS:, 3 * LANES:]
        bcums = (pre_f, suf_b + log_fs[1])
        rests = (suf_f, pre_b - log_fs[1])
        chunk_slices = [slice(j * HGRN_CHUNK, (j + 1) * HGRN_CHUNK) for j in range(chunks)]
        qes, decs, o_ins, upds = [], [], [], []
        for d in range(2):
            bcum, rest, kk = bcums[d], rests[d], kks[d]
            vv = iv_ref[rows[d], :].astype(BF16)
            qe = (q_ref[rows[d], :].astype(F32) * jnp.exp(bcum)).astype(BF16)
            ke = (kk * jnp.exp(-bcum)).astype(BF16)
            kd = (kk * jnp.exp(rest)).astype(BF16)
            att = jnp.where(mask_ref[d] > 0.5, _dot_nt(qe, ke), 0.0)
            qes.append(qe)
            o_ins.append(_dot(att.astype(BF16), vv))
            decs.append(jnp.exp(bcum + rest))
            upds.append([_dot_tn(vv[ch], kd[ch]) for ch in chunk_slices])
        entering = []
        for d in range(2):
            st = st_ref[d]
            states = [None] * chunks
            for j in (range(chunks) if d == 0 else reversed(range(chunks))):
                states[j] = st.astype(BF16)
                st = decs[d][j * HGRN_CHUNK:j * HGRN_CHUNK + 1, :] * st + upds[d][j]
            st_ref[d] = st
            entering.append(states)
        for d in range(2):
            pieces = [_dot_nt(qes[d][ch], entering[d][j]) for j, ch in enumerate(chunk_slices)]
            o_ref[d, rows[d], :] = o_ins[d] + jnp.concatenate(pieces, axis=0)
        return carry

    lax.fori_loop(0, n_blocks, vector_phase, 0)

    def scan_phase(t, carry):
        hf, pf, hb, pb = carry
        tb = MIX_ROWS - 1 - t
        af = a_ref[0, pl.ds(t, SUBLANES, stride=SEG_PITCH), :]
        hf = af * hf + u_ref[0, pl.ds(t, SUBLANES, stride=SEG_PITCH), :]
        pf = pf * af
        hloc_ref[0, pl.ds(t, SUBLANES, stride=SEG_PITCH), :] = hf
        pcum_ref[0, pl.ds(t, SUBLANES, stride=SEG_PITCH), :] = pf
        ab = a_ref[1, pl.ds(tb, SUBLANES, stride=SEG_PITCH), :]
        hb = ab * hb + u_ref[1, pl.ds(tb, SUBLANES, stride=SEG_PITCH), :]
        pb = pb * ab
        hloc_ref[1, pl.ds(tb, SUBLANES, stride=SEG_PITCH), :] = hb
        pcum_ref[1, pl.ds(tb, SUBLANES, stride=SEG_PITCH), :] = pb
        return hf, pf, hb, pb

    ones8 = jnp.ones((SUBLANES, LANES), F32)
    hf, pf, hb, pb = lax.fori_loop(0, MIX_ROWS, scan_phase, (zeros8, ones8, zeros8, ones8), unroll=SCAN_UNROLL)
    sub = lax.broadcasted_iota(jnp.int32, (SUBLANES, LANES), 0)
    cin_ref[0] = jnp.where(sub >= 1, _shift_down(_sublane_linear_scan(pf, hf, False), 1), 0.0)
    cin_ref[1] = jnp.where(sub < SUBLANES - 1, _shift_up(_sublane_linear_scan(pb, hb, True), 1), 0.0)

    def out_phase(c, carry):
        rows = pl.ds(pl.multiple_of(c * MIX_ROWS, MIX_ROWS), MIX_ROWS)
        seg_rows = pl.ds(pl.multiple_of(c * SEG_PITCH, SUBLANES), MIX_ROWS)
        h = (hloc_ref[0, seg_rows, :] + pcum_ref[0, seg_rows, :] * cin_ref[0, pl.ds(c, 1), :]
             + hloc_ref[1, seg_rows, :] + pcum_ref[1, seg_rows, :] * cin_ref[1, pl.ds(c, 1), :])
        y_a = _gelu_tanh(ga_ref[rows, :].astype(F32)) * h
        gv = g_ref[rows, :].astype(F32)
        y_b = _rms(o_ref[0, rows, :] + o_ref[1, rows, :], ng_ref[...]) * (gv * _sigmoid(gv))
        y_ref[rows, pl.ds(0, LANES)] = y_a.astype(y_ref.dtype)
        y_ref[rows, pl.ds(LANES, LANES)] = y_b.astype(y_ref.dtype)
        return carry

    lax.fori_loop(0, n_blocks, out_phase, 0)


def _rec_mixer(layer, proj, batch, seq, conv_w, conv_b, w_gate, b_gate, lam, lb_logits, hg_norm):
    nh = A_HEADS
    col = lambda off: pl.BlockSpec((seq, LANES), lambda b, h, off=off: (b, off + h))
    in_specs = [col(0), col(nh), col(2 * nh), col(3 * nh), col(4 * nh), col(5 * nh), col(6 * nh),
                pl.BlockSpec((CONV_WIDTH, LANES), lambda b, h: (0, h)),
                pl.BlockSpec((1, LANES), lambda b, h: (0, h)),
                pl.BlockSpec((1, LANES, 4 * LANES), lambda b, h: (h, 0, 0)),
                pl.BlockSpec((1, 1, 4 * LANES), lambda b, h: (h, 0, 0)),
                pl.BlockSpec((2, LANES), lambda b, h: (0, h)),
                pl.BlockSpec((2, N_REC_LAYERS, LANES), lambda b, h: (0, 0, h)),
                pl.BlockSpec((1, LANES), lambda b, h: (0, h)),
                pl.BlockSpec((2 * MIX_ROWS, MIX_ROWS), lambda b, h: (0, 0)),
                pl.BlockSpec((2, MIX_ROWS, MIX_ROWS), lambda b, h: (0, 0, 0))]
    seg_total = (seq // MIX_ROWS) * SEG_PITCH
    return pl.pallas_call(
        functools.partial(_rec_mixer_kernel, layer, seq),
        grid=(batch, nh),
        in_specs=in_specs,
        out_specs=pl.BlockSpec((seq, 2 * LANES), lambda b, h: (b, h)),
        out_shape=jax.ShapeDtypeStruct((batch * seq, D_A + D_B), BF16),
        scratch_shapes=[
            pltpu.VMEM((seq + 2 * SUBLANES, LANES), F32),
            pltpu.VMEM((2, seg_total, LANES), F32),
            pltpu.VMEM((2, seg_total, LANES), F32),
            pltpu.VMEM((2, seg_total, LANES), F32),
            pltpu.VMEM((2, seg_total, LANES), F32),
            pltpu.VMEM((2, SUBLANES, LANES), F32),
            pltpu.VMEM((2, seq, LANES), F32),
            pltpu.VMEM((2, LANES, LANES), F32),
        ],
        compiler_params=pltpu.CompilerParams(
            dimension_semantics=("parallel", "parallel"), vmem_limit_bytes=VMEM_LIMIT),
        name="rec_mixer",
    )(proj, proj, proj, proj, proj, proj, proj,
      conv_w, conv_b.reshape(1, D_A), w_gate, b_gate, lam, lb_logits, hg_norm.reshape(1, D_B),
      _chunk_sum_matrix(), _chunk_causal_masks())


def _rope(t, cos, sin):
    w = t.shape[1]
    lane = lax.broadcasted_iota(jnp.int32, t.shape, 1)
    rot = jnp.where(lane % HEAD_DIM < HEAD_DIM // 2,
                    pltpu.roll(t, w - HEAD_DIM // 2, 1), pltpu.roll(t, HEAD_DIM // 2, 1))
    return t * cos + rot * sin


def _attn_kernel(seq, q_ref, k_ref, v_ref, cos_ref, sin_ref, sink_ref, o_ref, ks_ref, vt_ref):
    n = pl.program_id(1)
    span = 3 * QBLOCK
    heads_per_tile = LANES // HEAD_DIM
    tiles_per_group = GROUP // heads_per_tile

    @pl.when(n == 0)
    def _():
        def prep(j, carry):
            r = pl.ds(pl.multiple_of(j * QBLOCK, QBLOCK), QBLOCK)
            cos_b, sin_b = cos_ref[r, :], sin_ref[r, :]
            upper = lax.broadcasted_iota(jnp.int32, (QBLOCK, LANES), 1) >= HEAD_DIM
            for t in range(N_KV_HEADS // heads_per_tile):
                kt = _rope(k_ref[r, pl.ds(t * LANES, LANES)].astype(F32), cos_b, sin_b)
                vt = v_ref[r, pl.ds(t * LANES, LANES)].astype(F32)
                for hh in range(heads_per_tile):
                    g = t * heads_per_tile + hh
                    own = upper if hh else jnp.logical_not(upper)
                    k_same = jnp.where(own, kt, 0.0)
                    v_same = jnp.where(own, vt, 0.0)
                    k_swap = pltpu.roll(k_same, HEAD_DIM, 1)
                    v_swap = pltpu.roll(v_same, HEAD_DIM, 1)
                    k_halves = (k_same, k_swap) if hh == 0 else (k_swap, k_same)
                    v_halves = (v_same, v_swap) if hh == 0 else (v_swap, v_same)
                    for p in range(heads_per_tile):
                        ks_ref[p, g, r, :] = k_halves[p].astype(BF16)
                        vt_ref[p, g, j] = v_halves[p].T.astype(BF16)
            return carry

        lax.fori_loop(0, seq // QBLOCK, prep, 0)

    q0 = pl.multiple_of(n * QBLOCK, QBLOCK)
    start = pl.multiple_of(jnp.clip(q0 - QBLOCK, 0, seq - span), QBLOCK)
    first_block = start // QBLOCK
    cos_q = cos_ref[pl.ds(q0, QBLOCK), :]
    sin_q = sin_ref[pl.ds(q0, QBLOCK), :]
    kpos = start + lax.broadcasted_iota(jnp.int32, (span, LANES), 0)
    qpos = q0 + lax.broadcasted_iota(jnp.int32, (span, LANES), 1)
    bias = jnp.where(jnp.abs(qpos - kpos) <= WINDOW, 0.0, -jnp.inf)
    bias = jnp.concatenate([bias] * tiles_per_group, axis=1)
    second_tile = lax.broadcasted_iota(jnp.int32, (1, tiles_per_group * LANES), 1) >= LANES
    scale = HEAD_DIM ** -0.5
    for g in range(N_KV_HEADS):
        qs = jnp.concatenate(
            [(_rope(q_ref[:, pl.ds((g * tiles_per_group + t) * LANES, LANES)].astype(F32), cos_q, sin_q)
              * scale).astype(BF16)
             for t in range(tiles_per_group)], axis=0)
        acc = jnp.zeros((LANES, tiles_per_group * QBLOCK), F32)
        for p in range(heads_per_tile):
            sc = _dot_nt(ks_ref[p, g, pl.ds(start, span), :], qs) + bias
            sink = jnp.where(second_tile, sink_ref[g * GROUP + heads_per_tile + p], sink_ref[g * GROUP + p])
            m = jnp.maximum(jnp.max(sc, axis=0, keepdims=True), sink)
            e = jnp.exp(sc - m)
            denom = jnp.sum(e, axis=0, keepdims=True) + jnp.exp(sink - m)
            pt = (e * (1.0 / denom)).astype(BF16)
            for j in range(span // QBLOCK):
                acc = acc + _dot(vt_ref[p, g, first_block + j], pt[j * QBLOCK:(j + 1) * QBLOCK, :])
        for t in range(tiles_per_group):
            o_ref[:, pl.ds((g * tiles_per_group + t) * LANES, LANES)] = (
                acc[:, t * QBLOCK:(t + 1) * QBLOCK].T.astype(o_ref.dtype))


def _attention(qkv, batch, seq, cos_t, sin_t, sinks):
    nb = seq // QBLOCK
    q_w = N_Q_HEADS * HEAD_DIM
    kv_w = N_KV_HEADS * HEAD_DIM
    return pl.pallas_call(
        functools.partial(_attn_kernel, seq),
        grid=(batch, nb),
        in_specs=[
            pl.BlockSpec((QBLOCK, q_w), lambda b, n: (b * nb + n, 0)),
            pl.BlockSpec((seq, kv_w), lambda b, n: (b, q_w // kv_w)),
            pl.BlockSpec((seq, kv_w), lambda b, n: (b, q_w // kv_w + 1)),
            pl.BlockSpec((seq, LANES), lambda b, n: (0, 0)),
            pl.BlockSpec((seq, LANES), lambda b, n: (0, 0)),
            pl.BlockSpec(memory_space=pltpu.SMEM),
        ],
        out_specs=pl.BlockSpec((QBLOCK, q_w), lambda b, n: (b * nb + n, 0)),
        out_shape=jax.ShapeDtypeStruct((batch * seq, q_w), BF16),
        scratch_shapes=[pltpu.VMEM((LANES // HEAD_DIM, N_KV_HEADS, seq, LANES), BF16),
                        pltpu.VMEM((LANES // HEAD_DIM, N_KV_HEADS, nb, LANES, QBLOCK), BF16)],
        compiler_params=pltpu.CompilerParams(
            dimension_semantics=("parallel", "arbitrary"), vmem_limit_bytes=VMEM_LIMIT),
        name="window_attention",
    )(qkv, qkv, qkv, cos_t, sin_t, sinks)


def _rope_tables(seq):
    pos = jnp.arange(seq, dtype=F32)
    inv_freq = ROPE_THETA ** (-jnp.arange(0, HEAD_DIM, 2, dtype=F32) / HEAD_DIM)
    ang = pos[:, None] * inv_freq[None, :]
    cos, sin = jnp.cos(ang), jnp.sin(ang)
    reps = LANES // HEAD_DIM
    return (jnp.tile(jnp.concatenate([cos, cos], axis=1), (1, reps)),
            jnp.tile(jnp.concatenate([-sin, sin], axis=1), (1, reps)))


def kernel(x, norm_g, rec_w_in, rec_conv_w, rec_conv_b, rg_w_r, rg_b_r, rg_w_i, rg_b_i, rg_lambda,
           hgrn_lb_logits, hgrn_norm_g, rec_w_out, att_w_qkv, att_sinks, att_w_o, mlp_w1, mlp_w2):
    batch, seq, d = x.shape
    cos_t, sin_t = _rope_tables(seq)
    h = x.reshape(batch * seq, d)
    for layer in range(DEPTH):
        g = norm_g[layer]
        if layer % 2 == 0:
            r = layer // 2
            proj = _norm_matmul(h, g[0], rec_w_in[r].astype(BF16), 512, BF16)
            w_gate = jnp.concatenate([rg_w_r[r, 0], rg_w_i[r, 0], rg_w_r[r, 1], rg_w_i[r, 1]], axis=-1).astype(BF16)
            b_gate = jnp.stack([rg_b_r[r, 0], rg_b_i[r, 0], rg_b_r[r, 1], rg_b_i[r, 1]], axis=0)
            b_gate = b_gate.reshape(4, A_HEADS, A_BLOCK).transpose(1, 0, 2).reshape(A_HEADS, 1, 4 * A_BLOCK)
            y = _rec_mixer(r, proj, batch, seq, rec_conv_w[r], rec_conv_b[r], w_gate, b_gate, rg_lambda[r],
                           hgrn_lb_logits, hgrn_norm_g[r])
            w_out = rec_w_out[r].reshape(2, A_HEADS, A_BLOCK, d).transpose(1, 0, 2, 3).reshape(D_A + D_B, d)
            h = _matmul_norm_res(y, w_out.astype(BF16), g[1], h)
        else:
            a = layer // 2
            qkv = _norm_matmul(h, g[0], att_w_qkv[a].astype(BF16), 512, BF16)
            o = _attention(qkv, batch, seq, cos_t, sin_t, att_sinks[a])
            h = _matmul_norm_res(o, att_w_o[a].astype(BF16), g[1], h)
        h = _mlp(h, g[2], mlp_w1[layer].astype(BF16), mlp_w2[layer].astype(BF16), g[3])
    return h.reshape(batch, seq, d)
```

```python
import functools

import jax
import jax.numpy as jnp
from jax import lax
from jax.experimental import pallas as pl
from jax.experimental.pallas import tpu as pltpu

D_MODEL = 1024
DEPTH = 4
N_REC_LAYERS = (DEPTH + 1) // 2
D_A = D_MODEL // 2
A_HEADS = 4
A_BLOCK = D_A // A_HEADS
CONV_WIDTH = 4
RGLRU_C = 8.0
D_B = D_MODEL // 2
B_HEADS = 4
B_DK = D_B // B_HEADS
HGRN_CHUNK = 32
REC_IN = 2 * D_A + 5 * D_B
HEAD_DIM = 64
N_Q_HEADS = D_MODEL // HEAD_DIM
N_KV_HEADS = 4
GROUP = N_Q_HEADS // N_KV_HEADS
WINDOW = 128
QBLOCK = 128
ROPE_THETA = 10000.0
QKV_OUT = (N_Q_HEADS + 2 * N_KV_HEADS) * HEAD_DIM
D_FF = 4 * D_MODEL
EPS = 1e-6

LANES = 128
SUBLANES = 8
ROW_TILE = 1024
FF_TILE = 1024
DOT_COLS = 1024
MIX_ROWS = 256
SEG_PITCH = MIX_ROWS + SUBLANES
SCAN_UNROLL = 8
VMEM_LIMIT = 52 * 1024 * 1024

BF16 = jnp.bfloat16
F32 = jnp.float32


def _rms(x, g):
    return x * lax.rsqrt(jnp.mean(x * x, axis=-1, keepdims=True) + EPS) * g


def _dot(a, b):
    return jnp.dot(a, b, preferred_element_type=F32)


def _dot_nt(a, b):
    return lax.dot_general(a, b, (((1,), (1,)), ((), ())), preferred_element_type=F32)


def _dot_tn(a, b):
    return lax.dot_general(a, b, (((0,), (0,)), ((), ())), preferred_element_type=F32)


def _norm_matmul_kernel(x_ref, g_ref, w_ref, o_ref):
    xn = _rms(x_ref[...], g_ref[...]).astype(BF16)
    n = o_ref.shape[1]
    for c0 in range(0, n, DOT_COLS):
        c1 = min(c0 + DOT_COLS, n)
        o_ref[:, c0:c1] = _dot(xn, w_ref[:, c0:c1]).astype(o_ref.dtype)


def _norm_matmul(x, g, w, out_dtype):
    m, k = x.shape
    n = w.shape[1]
    return pl.pallas_call(
        _norm_matmul_kernel,
        grid=(m // ROW_TILE,),
        in_specs=[
            pl.BlockSpec((ROW_TILE, k), lambda i: (i, 0)),
            pl.BlockSpec((1, k), lambda i: (0, 0)),
            pl.BlockSpec((k, n), lambda i: (0, 0)),
        ],
        out_specs=pl.BlockSpec((ROW_TILE, n), lambda i: (i, 0)),
        out_shape=jax.ShapeDtypeStruct((m, n), out_dtype),
        compiler_params=pltpu.CompilerParams(
            dimension_semantics=("parallel",), vmem_limit_bytes=VMEM_LIMIT),
        name="norm_matmul",
    )(x, g.reshape(1, k), w)


def _out_mlp_kernel(a_ref, wo_ref, g_mix_ref, h_ref, g_in_ref, w1_ref, w2_ref, g_out_ref, o_ref, xn_ref, acc_ref):
    j = pl.program_id(1)

    @pl.when(j == 0)
    def _():
        h1 = h_ref[...] + _rms(_dot(a_ref[...], wo_ref[...]), g_mix_ref[...])
        o_ref[...] = h1
        xn_ref[...] = _rms(h1, g_in_ref[...]).astype(BF16)
        acc_ref[...] = jnp.zeros_like(acc_ref)

    hid = jnp.maximum(_dot(xn_ref[...], w1_ref[...]), 0.0)
    acc_ref[...] += _dot((hid * hid).astype(BF16), w2_ref[...])

    @pl.when(j == pl.num_programs(1) - 1)
    def _():
        o_ref[...] = o_ref[...] + _rms(acc_ref[...], g_out_ref[...])


def _out_mlp(a, w_o, g_mix, h, g_in, w1, w2, g_out):
    m, d = h.shape
    k = a.shape[1]
    ff = w1.shape[1]
    vec = pl.BlockSpec((1, d), lambda i, j: (0, 0))
    return pl.pallas_call(
        _out_mlp_kernel,
        grid=(m // ROW_TILE, ff // FF_TILE),
        in_specs=[
            pl.BlockSpec((ROW_TILE, k), lambda i, j: (i, 0)),
            pl.BlockSpec((k, d), lambda i, j: (0, 0)),
            vec,
            pl.BlockSpec((ROW_TILE, d), lambda i, j: (i, 0)),
            vec,
            pl.BlockSpec((d, FF_TILE), lambda i, j: (0, j)),
            pl.BlockSpec((FF_TILE, d), lambda i, j: (j, 0)),
            vec,
        ],
        out_specs=pl.BlockSpec((ROW_TILE, d), lambda i, j: (i, 0)),
        out_shape=jax.ShapeDtypeStruct((m, d), F32),
        scratch_shapes=[pltpu.VMEM((ROW_TILE, d), BF16), pltpu.VMEM((ROW_TILE, d), F32)],
        compiler_params=pltpu.CompilerParams(
            dimension_semantics=("parallel", "arbitrary"), vmem_limit_bytes=VMEM_LIMIT),
        name="out_mlp",
    )(a, w_o, g_mix.reshape(1, d), h, g_in.reshape(1, d), w1, w2, g_out.reshape(1, d))


def _shift_down(x, k):
    return pltpu.roll(x, k, 0)


def _shift_up(x, k):
    return pltpu.roll(x, x.shape[0] - k, 0)


def _chunk_sum_matrix():
    i = jnp.arange(MIX_ROWS)[:, None]
    j = jnp.arange(MIX_ROWS)[None, :]
    same = (i // HGRN_CHUNK) == (j // HGRN_CHUNK)
    return jnp.concatenate([same & (j <= i), same & (j > i)], axis=0).astype(BF16)


def _chunk_causal_masks():
    i = jnp.arange(MIX_ROWS)[:, None]
    j = jnp.arange(MIX_ROWS)[None, :]
    same = (i // HGRN_CHUNK) == (j // HGRN_CHUNK)
    return jnp.stack([same & (j <= i), same & (j >= i)]).astype(F32)


def _sublane_linear_scan(a, u, reverse):
    pos = lax.broadcasted_iota(jnp.int32, a.shape, 0)
    k = 1
    while k < SUBLANES:
        if reverse:
            keep = pos < SUBLANES - k
            a_n, u_n = _shift_up(a, k), _shift_up(u, k)
        else:
            keep = pos >= k
            a_n, u_n = _shift_down(a, k), _shift_down(u, k)
        u = u + jnp.where(keep, a * u_n, 0.0)
        a = a * jnp.where(keep, a_n, 1.0)
        k *= 2
    return u


def _sigmoid(x):
    return 0.5 * jnp.tanh(0.5 * x) + 0.5


def _softplus(x):
    return jnp.maximum(x, 0.0) + jnp.log1p(jnp.exp(-jnp.abs(x)))


def _gelu_tanh(x):
    return 0.5 * x * (1.0 + jnp.tanh(0.7978845608028654 * (x + 0.044715 * (x * x * x))))


def _rec_mixer_kernel(layer, seq,
                      xa_ref, ga_ref, q_ref, zf_ref, zb_ref, iv_ref, g_ref,
                      cw_ref, cb_ref, wg_ref, bg_ref, lam_ref, lbl_ref, ng_ref, tri_ref, mask_ref,
                      y_ref,
                      xpad_ref, a_ref, u_ref, pcum_ref, hloc_ref, cin_ref, o_ref, st_ref):
    n_blocks = seq // MIX_ROWS
    chunks = MIX_ROWS // HGRN_CHUNK
    assert n_blocks == SUBLANES

    logits = lbl_ref[...]
    e = jnp.exp(logits - jnp.max(logits, axis=1, keepdims=True))
    s_lb = e / jnp.sum(e, axis=1, keepdims=True)
    lb = jnp.zeros((2, LANES), F32)
    for r in range(1, layer + 1):
        lb = lb + s_lb[:, r, :]
    sp = _softplus(-lam_ref[...])

    zeros8 = jnp.zeros((SUBLANES, LANES), F32)
    xpad_ref[pl.ds(0, SUBLANES), :] = zeros8
    xpad_ref[pl.ds(seq + SUBLANES, SUBLANES), :] = zeros8
    xpad_ref[pl.ds(SUBLANES, seq), :] = xa_ref[...].astype(F32)
    st_ref[...] = jnp.zeros_like(st_ref)

    def vector_phase(c, carry):
        r0 = pl.multiple_of(c * MIX_ROWS, MIX_ROWS)
        seg_rows = pl.ds(pl.multiple_of(c * SEG_PITCH, SUBLANES), MIX_ROWS)
        win = xpad_ref[pl.ds(r0, MIX_ROWS + 2 * SUBLANES), :]
        wrows = MIX_ROWS + 2 * SUBLANES
        xc = jnp.zeros((MIX_ROWS, LANES), F32) + cb_ref[...]
        for k in range(CONV_WIDTH):
            shift = (2 - k) % wrows
            shifted = (pltpu.roll(win, shift, 0) if shift else win)[SUBLANES:SUBLANES + MIX_ROWS]
            xc = xc + cw_ref[pl.ds(k, 1), :] * shifted
        gates = _dot(xc.astype(BF16), wg_ref[0]) + bg_ref[0]
        for d in range(2):
            r = _sigmoid(gates[:, (2 * d) * LANES:(2 * d + 1) * LANES])
            i = _sigmoid(gates[:, (2 * d + 1) * LANES:(2 * d + 2) * LANES])
            log_a = (-RGLRU_C) * r * sp[d:d + 1, :]
            a = jnp.exp(log_a)
            a_ref[d, seg_rows, :] = a
            u_ref[d, seg_rows, :] = jnp.sqrt(-jnp.tanh(log_a) * (1.0 + a * a)) * (i * xc)

        rows = [pl.ds(pl.multiple_of(blk * MIX_ROWS, MIX_ROWS), MIX_ROWS) for blk in (c, n_blocks - 1 - c)]
        log_fs, kks, parts = [], [], []
        for d, z_ref in enumerate((zf_ref, zb_ref)):
            sig = _sigmoid(z_ref[rows[d], :].astype(F32))
            lbd = lb[d:d + 1, :]
            log_f = jnp.log(lbd + (1.0 - lbd) * sig)
            log_fs.append(log_f)
            kks.append((1.0 - lbd) * (1.0 - sig))
            hi = log_f.astype(BF16)
            parts += [hi, (log_f - hi.astype(F32)).astype(BF16)]
        sums = _dot(tri_ref[...], jnp.concatenate(parts, axis=1))
        pre_f = sums[:MIX_ROWS, 0:LANES] + sums[:MIX_ROWS, LANES:2 * LANES]
        suf_f = sums[MIX_ROWS:, 0:LANES] + sums[MIX_ROWS:, LANES:2 * LANES]
        pre_b = sums[:MIX_ROWS, 2 * LANES:3 * LANES] + sums[:MIX_ROWS, 3 * LANES:]
        suf_b = sums[MIX_ROWS:, 2 * LANES:3 * LANES] + sums[MIX_ROWS:, 3 * LANES:]
        bcums = (pre_f, suf_b + log_fs[1])
        rests = (suf_f, pre_b - log_fs[1])
        chunk_slices = [slice(j * HGRN_CHUNK, (j + 1) * HGRN_CHUNK) for j in range(chunks)]
        qes, decs, o_ins, upds = [], [], [], []
        for d in range(2):
            bcum, rest, kk = bcums[d], rests[d], kks[d]
            vv = iv_ref[rows[d], :].astype(BF16)
            qe = (q_ref[rows[d], :].astype(F32) * jnp.exp(bcum)).astype(BF16)
            ke = (kk * jnp.exp(-bcum)).astype(BF16)
            kd = (kk * jnp.exp(rest)).astype(BF16)
            att = jnp.where(mask_ref[d] > 0.5, _dot_nt(qe, ke), 0.0)
            qes.append(qe)
            o_ins.append(_dot(att.astype(BF16), vv))
            decs.append(jnp.exp(bcum + rest))
            upds.append([_dot_tn(vv[ch], kd[ch]) for ch in chunk_slices])
        entering = []
        for d in range(2):
            st = st_ref[d]
            states = [None] * chunks
            for j in (range(chunks) if d == 0 else reversed(range(chunks))):
                states[j] = st.astype(BF16)
                st = decs[d][j * HGRN_CHUNK:j * HGRN_CHUNK + 1, :] * st + upds[d][j]
            st_ref[d] = st
            entering.append(states)
        for d in range(2):
            pieces = [_dot_nt(qes[d][ch], entering[d][j]) for j, ch in enumerate(chunk_slices)]
            o_ref[d, rows[d], :] = o_ins[d] + jnp.concatenate(pieces, axis=0)
        return carry

    lax.fori_loop(0, n_blocks, vector_phase, 0)

    def scan_phase(t, carry):
        hf, pf, hb, pb = carry
        tb = MIX_ROWS - 1 - t
        af = a_ref[0, pl.ds(t, SUBLANES, stride=SEG_PITCH), :]
        hf = af * hf + u_ref[0, pl.ds(t, SUBLANES, stride=SEG_PITCH), :]
        pf = pf * af
        hloc_ref[0, pl.ds(t, SUBLANES, stride=SEG_PITCH), :] = hf
        pcum_ref[0, pl.ds(t, SUBLANES, stride=SEG_PITCH), :] = pf
        ab = a_ref[1, pl.ds(tb, SUBLANES, stride=SEG_PITCH), :]
        hb = ab * hb + u_ref[1, pl.ds(tb, SUBLANES, stride=SEG_PITCH), :]
        pb = pb * ab
        hloc_ref[1, pl.ds(tb, SUBLANES, stride=SEG_PITCH), :] = hb
        pcum_ref[1, pl.ds(tb, SUBLANES, stride=SEG_PITCH), :] = pb
        return hf, pf, hb, pb

    ones8 = jnp.ones((SUBLANES, LANES), F32)
    hf, pf, hb, pb = lax.fori_loop(0, MIX_ROWS, scan_phase, (zeros8, ones8, zeros8, ones8), unroll=SCAN_UNROLL)
    sub = lax.broadcasted_iota(jnp.int32, (SUBLANES, LANES), 0)
    cin_ref[0] = jnp.where(sub >= 1, _shift_down(_sublane_linear_scan(pf, hf, False), 1), 0.0)
    cin_ref[1] = jnp.where(sub < SUBLANES - 1, _shift_up(_sublane_linear_scan(pb, hb, True), 1), 0.0)

    def out_phase(c, carry):
        rows = pl.ds(pl.multiple_of(c * MIX_ROWS, MIX_ROWS), MIX_ROWS)
        seg_rows = pl.ds(pl.multiple_of(c * SEG_PITCH, SUBLANES), MIX_ROWS)
        h = (hloc_ref[0, seg_rows, :] + pcum_ref[0, seg_rows, :] * cin_ref[0, pl.ds(c, 1), :]
             + hloc_ref[1, seg_rows, :] + pcum_ref[1, seg_rows, :] * cin_ref[1, pl.ds(c, 1), :])
        y_a = _gelu_tanh(ga_ref[rows, :].astype(F32)) * h
        gv = g_ref[rows, :].astype(F32)
        y_b = _rms(o_ref[0, rows, :] + o_ref[1, rows, :], ng_ref[...]) * (gv * _sigmoid(gv))
        y_ref[rows, pl.ds(0, LANES)] = y_a.astype(y_ref.dtype)
        y_ref[rows, pl.ds(LANES, LANES)] = y_b.astype(y_ref.dtype)
        return carry

    lax.fori_loop(0, n_blocks, out_phase, 0)


def _rec_mixer(layer, proj, batch, seq, conv_w, conv_b, w_gate, b_gate, lam, lb_logits, hg_norm):
    nh = A_HEADS
    col = lambda off: pl.BlockSpec((seq, LANES), lambda b, h, off=off: (b, off + h))
    in_specs = [col(0), col(nh), col(2 * nh), col(3 * nh), col(4 * nh), col(5 * nh), col(6 * nh),
                pl.BlockSpec((CONV_WIDTH, LANES), lambda b, h: (0, h)),
                pl.BlockSpec((1, LANES), lambda b, h: (0, h)),
                pl.BlockSpec((1, LANES, 4 * LANES), lambda b, h: (h, 0, 0)),
                pl.BlockSpec((1, 1, 4 * LANES), lambda b, h: (h, 0, 0)),
                pl.BlockSpec((2, LANES), lambda b, h: (0, h)),
                pl.BlockSpec((2, N_REC_LAYERS, LANES), lambda b, h: (0, 0, h)),
                pl.BlockSpec((1, LANES), lambda b, h: (0, h)),
                pl.BlockSpec((2 * MIX_ROWS, MIX_ROWS), lambda b, h: (0, 0)),
                pl.BlockSpec((2, MIX_ROWS, MIX_ROWS), lambda b, h: (0, 0, 0))]
    seg_total = (seq // MIX_ROWS) * SEG_PITCH
    return pl.pallas_call(
        functools.partial(_rec_mixer_kernel, layer, seq),
        grid=(batch, nh),
        in_specs=in_specs,
        out_specs=pl.BlockSpec((seq, 2 * LANES), lambda b, h: (b, h)),
        out_shape=jax.ShapeDtypeStruct((batch * seq, D_A + D_B), BF16),
        scratch_shapes=[
            pltpu.VMEM((seq + 2 * SUBLANES, LANES), F32),
            pltpu.VMEM((2, seg_total, LANES), F32),
            pltpu.VMEM((2, seg_total, LANES), F32),
            pltpu.VMEM((2, seg_total, LANES), F32),
            pltpu.VMEM((2, seg_total, LANES), F32),
            pltpu.VMEM((2, SUBLANES, LANES), F32),
            pltpu.VMEM((2, seq, LANES), F32),
            pltpu.VMEM((2, LANES, LANES), F32),
        ],
        compiler_params=pltpu.CompilerParams(
            dimension_semantics=("parallel", "parallel"), vmem_limit_bytes=VMEM_LIMIT),
        name="rec_mixer",
    )(proj, proj, proj, proj, proj, proj, proj,
      conv_w, conv_b.reshape(1, D_A), w_gate, b_gate, lam, lb_logits, hg_norm.reshape(1, D_B),
      _chunk_sum_matrix(), _chunk_causal_masks())


def _rope(t, cos, sin):
    w = t.shape[1]
    lane = lax.broadcasted_iota(jnp.int32, t.shape, 1)
    rot = jnp.where(lane % HEAD_DIM < HEAD_DIM // 2,
                    pltpu.roll(t, w - HEAD_DIM // 2, 1), pltpu.roll(t, HEAD_DIM // 2, 1))
    return t * cos + rot * sin


def _attn_kernel(seq, q_ref, k_ref, v_ref, cos_ref, sin_ref, sink_ref, o_ref, ks_ref, vt_ref):
    n = pl.program_id(1)
    span = 3 * QBLOCK
    heads_per_tile = LANES // HEAD_DIM
    tiles_per_group = GROUP // heads_per_tile

    @pl.when(n == 0)
    def _():
        def prep(j, carry):
            r = pl.ds(pl.multiple_of(j * QBLOCK, QBLOCK), QBLOCK)
            cos_b, sin_b = cos_ref[r, :], sin_ref[r, :]
            upper = lax.broadcasted_iota(jnp.int32, (QBLOCK, LANES), 1) >= HEAD_DIM
            for t in range(N_KV_HEADS // heads_per_tile):
                kt = _rope(k_ref[r, pl.ds(t * LANES, LANES)].astype(F32), cos_b, sin_b)
                vt = v_ref[r, pl.ds(t * LANES, LANES)].astype(F32)
                for hh in range(heads_per_tile):
                    g = t * heads_per_tile + hh
                    own = upper if hh else jnp.logical_not(upper)
                    k_same = jnp.where(own, kt, 0.0)
                    v_same = jnp.where(own, vt, 0.0)
                    k_swap = pltpu.roll(k_same, HEAD_DIM, 1)
                    v_swap = pltpu.roll(v_same, HEAD_DIM, 1)
                    k_halves = (k_same, k_swap) if hh == 0 else (k_swap, k_same)
                    v_halves = (v_same, v_swap) if hh == 0 else (v_swap, v_same)
                    for p in range(heads_per_tile):
                        ks_ref[p, g, r, :] = k_halves[p].astype(BF16)
                        vt_ref[p, g, j] = v_halves[p].T.astype(BF16)
            return carry

        lax.fori_loop(0, seq // QBLOCK, prep, 0)

    q0 = pl.multiple_of(n * QBLOCK, QBLOCK)
    start = pl.multiple_of(jnp.clip(q0 - QBLOCK, 0, seq - span), QBLOCK)
    first_block = start // QBLOCK
    cos_q = cos_ref[pl.ds(q0, QBLOCK), :]
    sin_q = sin_ref[pl.ds(q0, QBLOCK), :]
    kpos = start + lax.broadcasted_iota(jnp.int32, (span, LANES), 0)
    qpos = q0 + lax.broadcasted_iota(jnp.int32, (span, LANES), 1)
    bias = jnp.where(jnp.abs(qpos - kpos) <= WINDOW, 0.0, -jnp.inf)
    bias = jnp.concatenate([bias] * tiles_per_group, axis=1)
    second_tile = lax.broadcasted_iota(jnp.int32, (1, tiles_per_group * LANES), 1) >= LANES
    scale = HEAD_DIM ** -0.5
    for g in range(N_KV_HEADS):
        qs = jnp.concatenate(
            [(_rope(q_ref[:, pl.ds((g * tiles_per_group + t) * LANES, LANES)].astype(F32), cos_q, sin_q)
              * scale).astype(BF16)
             for t in range(tiles_per_group)], axis=0)
        acc = jnp.zeros((LANES, tiles_per_group * QBLOCK), F32)
        for p in range(heads_per_tile):
            sc = _dot_nt(ks_ref[p, g, pl.ds(start, span), :], qs) + bias
            sink = jnp.where(second_tile, sink_ref[g * GROUP + heads_per_tile + p], sink_ref[g * GROUP + p])
            m = jnp.maximum(jnp.max(sc, axis=0, keepdims=True), sink)
            e = jnp.exp(sc - m)
            denom = jnp.sum(e, axis=0, keepdims=True) + jnp.exp(sink - m)
            pt = (e * (1.0 / denom)).astype(BF16)
            for j in range(span // QBLOCK):
                acc = acc + _dot(vt_ref[p, g, first_block + j], pt[j * QBLOCK:(j + 1) * QBLOCK, :])
        for t in range(tiles_per_group):
            o_ref[:, pl.ds((g * tiles_per_group + t) * LANES, LANES)] = (
                acc[:, t * QBLOCK:(t + 1) * QBLOCK].T.astype(o_ref.dtype))


def _attention(qkv, batch, seq, cos_t, sin_t, sinks):
    nb = seq // QBLOCK
    q_w = N_Q_HEADS * HEAD_DIM
    kv_w = N_KV_HEADS * HEAD_DIM
    return pl.pallas_call(
        functools.partial(_attn_kernel, seq),
        grid=(batch, nb),
        in_specs=[
            pl.BlockSpec((QBLOCK, q_w), lambda b, n: (b * nb + n, 0)),
            pl.BlockSpec((seq, kv_w), lambda b, n: (b, q_w // kv_w)),
            pl.BlockSpec((seq, kv_w), lambda b, n: (b, q_w // kv_w + 1)),
            pl.BlockSpec((seq, LANES), lambda b, n: (0, 0)),
            pl.BlockSpec((seq, LANES), lambda b, n: (0, 0)),
            pl.BlockSpec(memory_space=pltpu.SMEM),
        ],
        out_specs=pl.BlockSpec((QBLOCK, q_w), lambda b, n: (b * nb + n, 0)),
        out_shape=jax.ShapeDtypeStruct((batch * seq, q_w), BF16),
        scratch_shapes=[pltpu.VMEM((LANES // HEAD_DIM, N_KV_HEADS, seq, LANES), BF16),
                        pltpu.VMEM((LANES // HEAD_DIM, N_KV_HEADS, nb, LANES, QBLOCK), BF16)],
        compiler_params=pltpu.CompilerParams(
            dimension_semantics=("parallel", "arbitrary"), vmem_limit_bytes=VMEM_LIMIT),
        name="window_attention",
    )(qkv, qkv, qkv, cos_t, sin_t, sinks)


def _rope_tables(seq):
    pos = jnp.arange(seq, dtype=F32)
    inv_freq = ROPE_THETA ** (-jnp.arange(0, HEAD_DIM, 2, dtype=F32) / HEAD_DIM)
    ang = pos[:, None] * inv_freq[None, :]
    cos, sin = jnp.cos(ang), jnp.sin(ang)
    reps = LANES // HEAD_DIM
    return (jnp.tile(jnp.concatenate([cos, cos], axis=1), (1, reps)),
            jnp.tile(jnp.concatenate([-sin, sin], axis=1), (1, reps)))


def kernel(x, norm_g, rec_w_in, rec_conv_w, rec_conv_b, rg_w_r, rg_b_r, rg_w_i, rg_b_i, rg_lambda,
           hgrn_lb_logits, hgrn_norm_g, rec_w_out, att_w_qkv, att_sinks, att_w_o, mlp_w1, mlp_w2):
    batch, seq, d = x.shape
    cos_t, sin_t = _rope_tables(seq)
    h = x.reshape(batch * seq, d)
    for layer in range(DEPTH):
        g = norm_g[layer]
        if layer % 2 == 0:
            r = layer // 2
            proj = _norm_matmul(h, g[0], rec_w_in[r].astype(BF16), BF16)
            w_gate = jnp.concatenate([rg_w_r[r, 0], rg_w_i[r, 0], rg_w_r[r, 1], rg_w_i[r, 1]], axis=-1).astype(BF16)
            b_gate = jnp.stack([rg_b_r[r, 0], rg_b_i[r, 0], rg_b_r[r, 1], rg_b_i[r, 1]], axis=0)
            b_gate = b_gate.reshape(4, A_HEADS, A_BLOCK).transpose(1, 0, 2).reshape(A_HEADS, 1, 4 * A_BLOCK)
            y = _rec_mixer(r, proj, batch, seq, rec_conv_w[r], rec_conv_b[r], w_gate, b_gate, rg_lambda[r],
                           hgrn_lb_logits, hgrn_norm_g[r])
            w_out = rec_w_out[r].reshape(2, A_HEADS, A_BLOCK, d).transpose(1, 0, 2, 3).reshape(D_A + D_B, d)
        else:
            a = layer // 2
            qkv = _norm_matmul(h, g[0], att_w_qkv[a].astype(BF16), BF16)
            y = _attention(qkv, batch, seq, cos_t, sin_t, att_sinks[a])
            w_out = att_w_o[a]
        h = _out_mlp(y, w_out.astype(BF16), g[1], h, g[2],
                     mlp_w1[layer].astype(BF16), mlp_w2[layer].astype(BF16), g[3])
    return h.reshape(batch, seq, d)
```

```python
import functools

import jax
import jax.numpy as jnp
from jax import lax
from jax.experimental import pallas as pl
from jax.experimental.pallas import tpu as pltpu

D_MODEL = 1024
DEPTH = 4
N_REC_LAYERS = (DEPTH + 1) // 2
D_A = D_MODEL // 2
A_HEADS = 4
A_BLOCK = D_A // A_HEADS
CONV_WIDTH = 4
RGLRU_C = 8.0
D_B = D_MODEL // 2
B_HEADS = 4
B_DK = D_B // B_HEADS
HGRN_CHUNK = 32
REC_IN = 2 * D_A + 5 * D_B
HEAD_DIM = 64
N_Q_HEADS = D_MODEL // HEAD_DIM
N_KV_HEADS = 4
GROUP = N_Q_HEADS // N_KV_HEADS
WINDOW = 128
QBLOCK = 128
ROPE_THETA = 10000.0
QKV_OUT = (N_Q_HEADS + 2 * N_KV_HEADS) * HEAD_DIM
D_FF = 4 * D_MODEL
EPS = 1e-6

LANES = 128
SUBLANES = 8
ROW_TILE = 1024
MLP_ROW_TILE = 512
DOT_COLS = 1024
MIX_ROWS = 256
SEG_PITCH = MIX_ROWS + SUBLANES
SCAN_UNROLL = 8
VMEM_LIMIT = 52 * 1024 * 1024

BF16 = jnp.bfloat16
F32 = jnp.float32


def _rms(x, g):
    return x * lax.rsqrt(jnp.mean(x * x, axis=-1, keepdims=True) + EPS) * g


def _dot(a, b):
    return jnp.dot(a, b, preferred_element_type=F32)


def _dot_nt(a, b):
    return lax.dot_general(a, b, (((1,), (1,)), ((), ())), preferred_element_type=F32)


def _dot_tn(a, b):
    return lax.dot_general(a, b, (((0,), (0,)), ((), ())), preferred_element_type=F32)


def _norm_matmul_kernel(x_ref, g_ref, w_ref, o_ref):
    xn = _rms(x_ref[...], g_ref[...]).astype(BF16)
    n = o_ref.shape[1]
    for c0 in range(0, n, DOT_COLS):
        c1 = min(c0 + DOT_COLS, n)
        o_ref[:, c0:c1] = _dot(xn, w_ref[:, c0:c1]).astype(o_ref.dtype)


def _norm_matmul(x, g, w, out_dtype):
    m, k = x.shape
    n = w.shape[1]
    return pl.pallas_call(
        _norm_matmul_kernel,
        grid=(m // ROW_TILE,),
        in_specs=[
            pl.BlockSpec((ROW_TILE, k), lambda i: (i, 0)),
            pl.BlockSpec((1, k), lambda i: (0, 0)),
            pl.BlockSpec((k, n), lambda i: (0, 0)),
        ],
        out_specs=pl.BlockSpec((ROW_TILE, n), lambda i: (i, 0)),
        out_shape=jax.ShapeDtypeStruct((m, n), out_dtype),
        compiler_params=pltpu.CompilerParams(
            dimension_semantics=("parallel",), vmem_limit_bytes=VMEM_LIMIT),
        name="norm_matmul",
    )(x, g.reshape(1, k), w)


def _out_mlp_kernel(a_ref, wo_ref, g_mix_ref, h_ref, g_in_ref, w1_ref, w2_ref, g_out_ref, o_ref):
    h1 = h_ref[...] + _rms(_dot(a_ref[...], wo_ref[...]), g_mix_ref[...])
    xn = _rms(h1, g_in_ref[...]).astype(BF16)
    ff = w1_ref.shape[1]
    acc = None
    for c0 in range(0, ff, DOT_COLS):
        hid = jnp.maximum(_dot(xn, w1_ref[:, c0:c0 + DOT_COLS]), 0.0)
        part = _dot((hid * hid).astype(BF16), w2_ref[c0:c0 + DOT_COLS, :])
        acc = part if acc is None else acc + part
    o_ref[...] = h1 + _rms(acc, g_out_ref[...])


def _out_mlp(a, w_o, g_mix, h, g_in, w1, w2, g_out):
    m, d = h.shape
    k = a.shape[1]
    ff = w1.shape[1]
    vec = pl.BlockSpec((1, d), lambda i: (0, 0))
    resident = lambda shape: pl.BlockSpec(shape, lambda i: (0, 0), pipeline_mode=pl.Buffered(1))
    return pl.pallas_call(
        _out_mlp_kernel,
        grid=(m // MLP_ROW_TILE,),
        in_specs=[
            pl.BlockSpec((MLP_ROW_TILE, k), lambda i: (i, 0)),
            resident((k, d)),
            vec,
            pl.BlockSpec((MLP_ROW_TILE, d), lambda i: (i, 0)),
            vec,
            resident((d, ff)),
            resident((ff, d)),
            vec,
        ],
        out_specs=pl.BlockSpec((MLP_ROW_TILE, d), lambda i: (i, 0)),
        out_shape=jax.ShapeDtypeStruct((m, d), F32),
        compiler_params=pltpu.CompilerParams(
            dimension_semantics=("parallel",), vmem_limit_bytes=VMEM_LIMIT),
        name="out_mlp",
    )(a, w_o, g_mix.reshape(1, d), h, g_in.reshape(1, d), w1, w2, g_out.reshape(1, d))


def _shift_down(x, k):
    return pltpu.roll(x, k, 0)


def _shift_up(x, k):
    return pltpu.roll(x, x.shape[0] - k, 0)


def _chunk_sum_matrix():
    i = jnp.arange(MIX_ROWS)[:, None]
    j = jnp.arange(MIX_ROWS)[None, :]
    same = (i // HGRN_CHUNK) == (j // HGRN_CHUNK)
    return jnp.concatenate([same & (j <= i), same & (j > i)], axis=0).astype(BF16)


def _chunk_causal_masks():
    i = jnp.arange(MIX_ROWS)[:, None]
    j = jnp.arange(MIX_ROWS)[None, :]
    same = (i // HGRN_CHUNK) == (j // HGRN_CHUNK)
    return jnp.stack([same & (j <= i), same & (j >= i)]).astype(F32)


def _sublane_linear_scan(a, u, reverse):
    pos = lax.broadcasted_iota(jnp.int32, a.shape, 0)
    k = 1
    while k < SUBLANES:
        if reverse:
            keep = pos < SUBLANES - k
            a_n, u_n = _shift_up(a, k), _shift_up(u, k)
        else:
            keep = pos >= k
            a_n, u_n = _shift_down(a, k), _shift_down(u, k)
        u = u + jnp.where(keep, a * u_n, 0.0)
        a = a * jnp.where(keep, a_n, 1.0)
        k *= 2
    return u


def _sigmoid(x):
    return 0.5 * jnp.tanh(0.5 * x) + 0.5


def _softplus(x):
    return jnp.maximum(x, 0.0) + jnp.log1p(jnp.exp(-jnp.abs(x)))


def _gelu_tanh(x):
    return 0.5 * x * (1.0 + jnp.tanh(0.7978845608028654 * (x + 0.044715 * (x * x * x))))


def _rec_mixer_kernel(layer, seq,
                      xa_ref, ga_ref, q_ref, zf_ref, zb_ref, iv_ref, g_ref,
                      cw_ref, cb_ref, wg_ref, bg_ref, lam_ref, lbl_ref, ng_ref, tri_ref, mask_ref,
                      y_ref,
                      xpad_ref, a_ref, u_ref, pcum_ref, hloc_ref, cin_ref, o_ref, st_ref):
    n_blocks = seq // MIX_ROWS
    chunks = MIX_ROWS // HGRN_CHUNK
    assert n_blocks == SUBLANES

    logits = lbl_ref[...]
    e = jnp.exp(logits - jnp.max(logits, axis=1, keepdims=True))
    s_lb = e / jnp.sum(e, axis=1, keepdims=True)
    lb = jnp.zeros((2, LANES), F32)
    for r in range(1, layer + 1):
        lb = lb + s_lb[:, r, :]
    sp = _softplus(-lam_ref[...])

    zeros8 = jnp.zeros((SUBLANES, LANES), F32)
    xpad_ref[pl.ds(0, SUBLANES), :] = zeros8
    xpad_ref[pl.ds(seq + SUBLANES, SUBLANES), :] = zeros8
    xpad_ref[pl.ds(SUBLANES, seq), :] = xa_ref[...].astype(F32)
    st_ref[...] = jnp.zeros_like(st_ref)

    def vector_phase(c, carry):
        r0 = pl.multiple_of(c * MIX_ROWS, MIX_ROWS)
        seg_rows = pl.ds(pl.multiple_of(c * SEG_PITCH, SUBLANES), MIX_ROWS)
        win = xpad_ref[pl.ds(r0, MIX_ROWS + 2 * SUBLANES), :]
        wrows = MIX_ROWS + 2 * SUBLANES
        xc = jnp.zeros((MIX_ROWS, LANES), F32) + cb_ref[...]
        for k in range(CONV_WIDTH):
            shift = (2 - k) % wrows
            shifted = (pltpu.roll(win, shift, 0) if shift else win)[SUBLANES:SUBLANES + MIX_ROWS]
            xc = xc + cw_ref[pl.ds(k, 1), :] * shifted
        gates = _dot(xc.astype(BF16), wg_ref[0]) + bg_ref[0]
        for d in range(2):
            r = _sigmoid(gates[:, (2 * d) * LANES:(2 * d + 1) * LANES])
            i = _sigmoid(gates[:, (2 * d + 1) * LANES:(2 * d + 2) * LANES])
            log_a = (-RGLRU_C) * r * sp[d:d + 1, :]
            a = jnp.exp(log_a)
            a_ref[d, seg_rows, :] = a
            u_ref[d, seg_rows, :] = jnp.sqrt(-jnp.tanh(log_a) * (1.0 + a * a)) * (i * xc)

        rows = [pl.ds(pl.multiple_of(blk * MIX_ROWS, MIX_ROWS), MIX_ROWS) for blk in (c, n_blocks - 1 - c)]
        log_fs, kks, parts = [], [], []
        for d, z_ref in enumerate((zf_ref, zb_ref)):
            sig = _sigmoid(z_ref[rows[d], :].astype(F32))
            lbd = lb[d:d + 1, :]
            log_f = jnp.log(lbd + (1.0 - lbd) * sig)
            log_fs.append(log_f)
            kks.append((1.0 - lbd) * (1.0 - sig))
            hi = log_f.astype(BF16)
            parts += [hi, (log_f - hi.astype(F32)).astype(BF16)]
        sums = _dot(tri_ref[...], jnp.concatenate(parts, axis=1))
        pre_f = sums[:MIX_ROWS, 0:LANES] + sums[:MIX_ROWS, LANES:2 * LANES]
        suf_f = sums[MIX_ROWS:, 0:LANES] + sums[MIX_ROWS:, LANES:2 * LANES]
        pre_b = sums[:MIX_ROWS, 2 * LANES:3 * LANES] + sums[:MIX_ROWS, 3 * LANES:]
        suf_b = sums[MIX_ROWS:, 2 * LANES:3 * LANES] + sums[MIX_ROWS:, 3 * LANES:]
        bcums = (pre_f, suf_b + log_fs[1])
        rests = (suf_f, pre_b - log_fs[1])
        chunk_slices = [slice(j * HGRN_CHUNK, (j + 1) * HGRN_CHUNK) for j in range(chunks)]
        qes, decs, o_ins, upds = [], [], [], []
        for d in range(2):
            bcum, rest, kk = bcums[d], rests[d], kks[d]
            vv = iv_ref[rows[d], :].astype(BF16)
            qe = (q_ref[rows[d], :].astype(F32) * jnp.exp(bcum)).astype(BF16)
            ke = (kk * jnp.exp(-bcum)).astype(BF16)
            kd = (kk * jnp.exp(rest)).astype(BF16)
            att = jnp.where(mask_ref[d] > 0.5, _dot_nt(qe, ke), 0.0)
            qes.append(qe)
            o_ins.append(_dot(att.astype(BF16), vv))
            decs.append([jnp.exp(bcum[ch.start:ch.start + 1, :] + rest[ch.start:ch.start + 1, :])
                         for ch in chunk_slices])
            upds.append([_dot_tn(vv[ch], kd[ch]) for ch in chunk_slices])
        entering = []
        for d in range(2):
            st = st_ref[d]
            states = [None] * chunks
            for j in (range(chunks) if d == 0 else reversed(range(chunks))):
                states[j] = st.astype(BF16)
                st = decs[d][j] * st + upds[d][j]
            st_ref[d] = st
            entering.append(states)
        for d in range(2):
            pieces = [_dot_nt(qes[d][ch], entering[d][j]) for j, ch in enumerate(chunk_slices)]
            o_ref[d, rows[d], :] = o_ins[d] + jnp.concatenate(pieces, axis=0)
        return carry

    lax.fori_loop(0, n_blocks, vector_phase, 0)

    def scan_phase(t, carry):
        hf, pf, hb, pb = carry
        tb = MIX_ROWS - 1 - t
        af = a_ref[0, pl.ds(t, SUBLANES, stride=SEG_PITCH), :]
        hf = af * hf + u_ref[0, pl.ds(t, SUBLANES, stride=SEG_PITCH), :]
        pf = pf * af
        hloc_ref[0, pl.ds(t, SUBLANES, stride=SEG_PITCH), :] = hf
        pcum_ref[0, pl.ds(t, SUBLANES, stride=SEG_PITCH), :] = pf
        ab = a_ref[1, pl.ds(tb, SUBLANES, stride=SEG_PITCH), :]
        hb = ab * hb + u_ref[1, pl.ds(tb, SUBLANES, stride=SEG_PITCH), :]
        pb = pb * ab
        hloc_ref[1, pl.ds(tb, SUBLANES, stride=SEG_PITCH), :] = hb
        pcum_ref[1, pl.ds(tb, SUBLANES, stride=SEG_PITCH), :] = pb
        return hf, pf, hb, pb

    ones8 = jnp.ones((SUBLANES, LANES), F32)
    hf, pf, hb, pb = lax.fori_loop(0, MIX_ROWS, scan_phase, (zeros8, ones8, zeros8, ones8), unroll=SCAN_UNROLL)
    sub = lax.broadcasted_iota(jnp.int32, (SUBLANES, LANES), 0)
    cin_ref[0] = jnp.where(sub >= 1, _shift_down(_sublane_linear_scan(pf, hf, False), 1), 0.0)
    cin_ref[1] = jnp.where(sub < SUBLANES - 1, _shift_up(_sublane_linear_scan(pb, hb, True), 1), 0.0)

    def out_phase(c, carry):
        rows = pl.ds(pl.multiple_of(c * MIX_ROWS, MIX_ROWS), MIX_ROWS)
        seg_rows = pl.ds(pl.multiple_of(c * SEG_PITCH, SUBLANES), MIX_ROWS)
        h = (hloc_ref[0, seg_rows, :] + pcum_ref[0, seg_rows, :] * cin_ref[0, pl.ds(c, 1), :]
             + hloc_ref[1, seg_rows, :] + pcum_ref[1, seg_rows, :] * cin_ref[1, pl.ds(c, 1), :])
        y_a = _gelu_tanh(ga_ref[rows, :].astype(F32)) * h
        gv = g_ref[rows, :].astype(F32)
        y_b = _rms(o_ref[0, rows, :] + o_ref[1, rows, :], ng_ref[...]) * (gv * _sigmoid(gv))
        y_ref[rows, pl.ds(0, LANES)] = y_a.astype(y_ref.dtype)
        y_ref[rows, pl.ds(LANES, LANES)] = y_b.astype(y_ref.dtype)
        return carry

    lax.fori_loop(0, n_blocks, out_phase, 0)


def _rec_mixer(layer, proj, batch, seq, conv_w, conv_b, w_gate, b_gate, lam, lb_logits, hg_norm):
    nh = A_HEADS
    col = lambda off: pl.BlockSpec((seq, LANES), lambda b, h, off=off: (b, off + h))
    in_specs = [col(0), col(nh), col(2 * nh), col(3 * nh), col(4 * nh), col(5 * nh), col(6 * nh),
                pl.BlockSpec((CONV_WIDTH, LANES), lambda b, h: (0, h)),
                pl.BlockSpec((1, LANES), lambda b, h: (0, h)),
                pl.BlockSpec((1, LANES, 4 * LANES), lambda b, h: (h, 0, 0)),
                pl.BlockSpec((1, 1, 4 * LANES), lambda b, h: (h, 0, 0)),
                pl.BlockSpec((2, LANES), lambda b, h: (0, h)),
                pl.BlockSpec((2, N_REC_LAYERS, LANES), lambda b, h: (0, 0, h)),
                pl.BlockSpec((1, LANES), lambda b, h: (0, h)),
                pl.BlockSpec((2 * MIX_ROWS, MIX_ROWS), lambda b, h: (0, 0)),
                pl.BlockSpec((2, MIX_ROWS, MIX_ROWS), lambda b, h: (0, 0, 0))]
    seg_total = (seq // MIX_ROWS) * SEG_PITCH
    return pl.pallas_call(
        functools.partial(_rec_mixer_kernel, layer, seq),
        grid=(batch, nh),
        in_specs=in_specs,
        out_specs=pl.BlockSpec((seq, 2 * LANES), lambda b, h: (b, h)),
        out_shape=jax.ShapeDtypeStruct((batch * seq, D_A + D_B), BF16),
        scratch_shapes=[
            pltpu.VMEM((seq + 2 * SUBLANES, LANES), F32),
            pltpu.VMEM((2, seg_total, LANES), F32),
            pltpu.VMEM((2, seg_total, LANES), F32),
            pltpu.VMEM((2, seg_total, LANES), F32),
            pltpu.VMEM((2, seg_total, LANES), F32),
            pltpu.VMEM((2, SUBLANES, LANES), F32),
            pltpu.VMEM((2, seq, LANES), F32),
            pltpu.VMEM((2, LANES, LANES), F32),
        ],
        compiler_params=pltpu.CompilerParams(
            dimension_semantics=("parallel", "parallel"), vmem_limit_bytes=VMEM_LIMIT),
        name="rec_mixer",
    )(proj, proj, proj, proj, proj, proj, proj,
      conv_w, conv_b.reshape(1, D_A), w_gate, b_gate, lam, lb_logits, hg_norm.reshape(1, D_B),
      _chunk_sum_matrix(), _chunk_causal_masks())


def _rope(t, cos, sin):
    w = t.shape[1]
    lane = lax.broadcasted_iota(jnp.int32, t.shape, 1)
    rot = jnp.where(lane % HEAD_DIM < HEAD_DIM // 2,
                    pltpu.roll(t, w - HEAD_DIM // 2, 1), pltpu.roll(t, HEAD_DIM // 2, 1))
    return t * cos + rot * sin


def _ones_row(p):
    return LANES - 1 if p == 0 else 0


def _attn_kernel(seq, q_ref, k_ref, v_ref, cos_ref, sin_ref, sink_ref, o_ref, ks_ref, vt_ref):
    n = pl.program_id(1)
    span = 3 * QBLOCK
    heads_per_tile = LANES // HEAD_DIM
    tiles_per_group = GROUP // heads_per_tile

    @pl.when(n == 0)
    def _():
        def prep(j, carry):
            r = pl.ds(pl.multiple_of(j * QBLOCK, QBLOCK), QBLOCK)
            cos_b, sin_b = cos_ref[r, :], sin_ref[r, :]
            upper = lax.broadcasted_iota(jnp.int32, (QBLOCK, LANES), 1) >= HEAD_DIM
            dim = lax.broadcasted_iota(jnp.int32, (LANES, QBLOCK), 0)
            for t in range(N_KV_HEADS // heads_per_tile):
                kt = _rope(k_ref[r, pl.ds(t * LANES, LANES)].astype(F32), cos_b, sin_b)
                vt = v_ref[r, pl.ds(t * LANES, LANES)].astype(F32).T
                for hh in range(heads_per_tile):
                    g = t * heads_per_tile + hh
                    k_same = jnp.where(upper if hh else jnp.logical_not(upper), kt, 0.0)
                    v_same = jnp.where((dim >= HEAD_DIM) if hh else (dim < HEAD_DIM), vt, 0.0)
                    k_swap = pltpu.roll(k_same, HEAD_DIM, 1)
                    v_swap = pltpu.roll(v_same, HEAD_DIM, 0)
                    k_halves = (k_same, k_swap) if hh == 0 else (k_swap, k_same)
                    v_halves = (v_same, v_swap) if hh == 0 else (v_swap, v_same)
                    for p in range(heads_per_tile):
                        ks_ref[p, g, r, :] = k_halves[p].astype(BF16)
                        vt_ref[p, g, j] = jnp.where(dim == _ones_row(p), 1.0, v_halves[p]).astype(BF16)
            return carry

        lax.fori_loop(0, seq // QBLOCK, prep, 0)

    q0 = pl.multiple_of(n * QBLOCK, QBLOCK)
    start = pl.multiple_of(jnp.clip(q0 - QBLOCK, 0, seq - span), QBLOCK)
    first_block = start // QBLOCK
    cos_q = cos_ref[pl.ds(q0, QBLOCK), :]
    sin_q = sin_ref[pl.ds(q0, QBLOCK), :]
    kpos = start + lax.broadcasted_iota(jnp.int32, (span, LANES), 0)
    qpos = q0 + lax.broadcasted_iota(jnp.int32, (span, LANES), 1)
    bias = jnp.where(jnp.abs(qpos - kpos) <= WINDOW, 0.0, -jnp.inf)
    bias = jnp.concatenate([bias] * tiles_per_group, axis=1)
    second_tile = lax.broadcasted_iota(jnp.int32, (1, tiles_per_group * LANES), 1) >= LANES
    scale = HEAD_DIM ** -0.5
    for g in range(N_KV_HEADS):
        qs = jnp.concatenate(
            [(_rope(q_ref[:, pl.ds((g * tiles_per_group + t) * LANES, LANES)].astype(F32), cos_q, sin_q)
              * scale).astype(BF16)
             for t in range(tiles_per_group)], axis=0)
        halves = []
        for p in range(heads_per_tile):
            sc = _dot_nt(ks_ref[p, g, pl.ds(start, span), :], qs) + bias
            sink = jnp.where(second_tile, sink_ref[g * GROUP + heads_per_tile + p], sink_ref[g * GROUP + p])
            m = jnp.maximum(jnp.max(sc, axis=0, keepdims=True), sink)
            e = jnp.exp((sc - m).astype(BF16))
            pv = jnp.zeros((LANES, tiles_per_group * QBLOCK), F32)
            for j in range(span // QBLOCK):
                pv = pv + _dot(vt_ref[p, g, first_block + j], e[j * QBLOCK:(j + 1) * QBLOCK, :])
            denom = pv[_ones_row(p):_ones_row(p) + 1, :] + jnp.exp(sink - m)
            halves.append(pv * (1.0 / denom))
        dim = lax.broadcasted_iota(jnp.int32, halves[0].shape, 0)
        acc = jnp.where(dim < HEAD_DIM, halves[0], halves[1])
        for t in range(tiles_per_group):
            o_ref[:, pl.ds((g * tiles_per_group + t) * LANES, LANES)] = (
                acc[:, t * QBLOCK:(t + 1) * QBLOCK].T.astype(o_ref.dtype))


def _attention(qkv, batch, seq, cos_t, sin_t, sinks):
    nb = seq // QBLOCK
    q_w = N_Q_HEADS * HEAD_DIM
    kv_w = N_KV_HEADS * HEAD_DIM
    return pl.pallas_call(
        functools.partial(_attn_kernel, seq),
        grid=(batch, nb),
        in_specs=[
            pl.BlockSpec((QBLOCK, q_w), lambda b, n: (b * nb + n, 0)),
            pl.BlockSpec((seq, kv_w), lambda b, n: (b, q_w // kv_w)),
            pl.BlockSpec((seq, kv_w), lambda b, n: (b, q_w // kv_w + 1)),
            pl.BlockSpec((seq, LANES), lambda b, n: (0, 0)),
            pl.BlockSpec((seq, LANES), lambda b, n: (0, 0)),
            pl.BlockSpec(memory_space=pltpu.SMEM),
        ],
        out_specs=pl.BlockSpec((QBLOCK, q_w), lambda b, n: (b * nb + n, 0)),
        out_shape=jax.ShapeDtypeStruct((batch * seq, q_w), BF16),
        scratch_shapes=[pltpu.VMEM((LANES // HEAD_DIM, N_KV_HEADS, seq, LANES), BF16),
                        pltpu.VMEM((LANES // HEAD_DIM, N_KV_HEADS, nb, LANES, QBLOCK), BF16)],
        compiler_params=pltpu.CompilerParams(
            dimension_semantics=("parallel", "arbitrary"), vmem_limit_bytes=VMEM_LIMIT),
        name="window_attention",
    )(qkv, qkv, qkv, cos_t, sin_t, sinks)


def _rope_tables(seq):
    pos = jnp.arange(seq, dtype=F32)
    inv_freq = ROPE_THETA ** (-jnp.arange(0, HEAD_DIM, 2, dtype=F32) / HEAD_DIM)
    ang = pos[:, None] * inv_freq[None, :]
    cos, sin = jnp.cos(ang), jnp.sin(ang)
    reps = LANES // HEAD_DIM
    return (jnp.tile(jnp.concatenate([cos, cos], axis=1), (1, reps)),
            jnp.tile(jnp.concatenate([-sin, sin], axis=1), (1, reps)))


def kernel(x, norm_g, rec_w_in, rec_conv_w, rec_conv_b, rg_w_r, rg_b_r, rg_w_i, rg_b_i, rg_lambda,
           hgrn_lb_logits, hgrn_norm_g, rec_w_out, att_w_qkv, att_sinks, att_w_o, mlp_w1, mlp_w2):
    batch, seq, d = x.shape
    cos_t, sin_t = _rope_tables(seq)
    h = x.reshape(batch * seq, d)
    for layer in range(DEPTH):
        g = norm_g[layer]
        if layer % 2 == 0:
            r = layer // 2
            proj = _norm_matmul(h, g[0], rec_w_in[r].astype(BF16), BF16)
            w_gate = jnp.concatenate([rg_w_r[r, 0], rg_w_i[r, 0], rg_w_r[r, 1], rg_w_i[r, 1]], axis=-1).astype(BF16)
            b_gate = jnp.stack([rg_b_r[r, 0], rg_b_i[r, 0], rg_b_r[r, 1], rg_b_i[r, 1]], axis=0)
            b_gate = b_gate.reshape(4, A_HEADS, A_BLOCK).transpose(1, 0, 2).reshape(A_HEADS, 1, 4 * A_BLOCK)
            y = _rec_mixer(r, proj, batch, seq, rec_conv_w[r], rec_conv_b[r], w_gate, b_gate, rg_lambda[r],
                           hgrn_lb_logits, hgrn_norm_g[r])
            w_out = rec_w_out[r].reshape(2, A_HEADS, A_BLOCK, d).transpose(1, 0, 2, 3).reshape(D_A + D_B, d)
        else:
            a = layer // 2
            qkv = _norm_matmul(h, g[0], att_w_qkv[a].astype(BF16), BF16)
            y = _attention(qkv, batch, seq, cos_t, sin_t, att_sinks[a])
            w_out = att_w_o[a]
        h = _out_mlp(y, w_out.astype(BF16), g[1], h, g[2],
                     mlp_w1[layer].astype(BF16), mlp_w2[layer].astype(BF16), g[3])
    return h.reshape(batch, seq, d)
```

```python
import functools

import jax
import jax.numpy as jnp
from jax import lax
from jax.experimental import pallas as pl
from jax.experimental.pallas import tpu as pltpu

D_MODEL = 1024
DEPTH = 4
N_REC_LAYERS = (DEPTH + 1) // 2
D_A = D_MODEL // 2
A_HEADS = 4
A_BLOCK = D_A // A_HEADS
CONV_WIDTH = 4
RGLRU_C = 8.0
D_B = D_MODEL // 2
B_HEADS = 4
B_DK = D_B // B_HEADS
HGRN_CHUNK = 32
REC_IN = 2 * D_A + 5 * D_B
HEAD_DIM = 64
N_Q_HEADS = D_MODEL // HEAD_DIM
N_KV_HEADS = 4
GROUP = N_Q_HEADS // N_KV_HEADS
WINDOW = 128
QBLOCK = 128
ROPE_THETA = 10000.0
QKV_OUT = (N_Q_HEADS + 2 * N_KV_HEADS) * HEAD_DIM
D_FF = 4 * D_MODEL
EPS = 1e-6

LANES = 128
SUBLANES = 8
ROW_TILE = 1024
MLP_ROW_TILE = 512
DOT_COLS = 1024
MIX_ROWS = 256
SEG_PITCH = MIX_ROWS + SUBLANES
SCAN_UNROLL = 8
VMEM_LIMIT = 52 * 1024 * 1024

BF16 = jnp.bfloat16
F32 = jnp.float32


def _rms(x, g):
    return x * lax.rsqrt(jnp.mean(x * x, axis=-1, keepdims=True) + EPS) * g


def _dot(a, b):
    return jnp.dot(a, b, preferred_element_type=F32)


def _dot_nt(a, b):
    return lax.dot_general(a, b, (((1,), (1,)), ((), ())), preferred_element_type=F32)


def _dot_tn(a, b):
    return lax.dot_general(a, b, (((0,), (0,)), ((), ())), preferred_element_type=F32)


def _norm_matmul_kernel(x_ref, g_ref, w_ref, o_ref):
    xn = _rms(x_ref[...], g_ref[...]).astype(BF16)
    n = o_ref.shape[1]
    for c0 in range(0, n, DOT_COLS):
        c1 = min(c0 + DOT_COLS, n)
        o_ref[:, c0:c1] = _dot(xn, w_ref[:, c0:c1]).astype(o_ref.dtype)


def _norm_matmul(x, g, w, out_dtype):
    m, k = x.shape
    n = w.shape[1]
    return pl.pallas_call(
        _norm_matmul_kernel,
        grid=(m // ROW_TILE,),
        in_specs=[
            pl.BlockSpec((ROW_TILE, k), lambda i: (i, 0)),
            pl.BlockSpec((1, k), lambda i: (0, 0)),
            pl.BlockSpec((k, n), lambda i: (0, 0)),
        ],
        out_specs=pl.BlockSpec((ROW_TILE, n), lambda i: (i, 0)),
        out_shape=jax.ShapeDtypeStruct((m, n), out_dtype),
        compiler_params=pltpu.CompilerParams(
            dimension_semantics=("parallel",), vmem_limit_bytes=VMEM_LIMIT),
        name="norm_matmul",
    )(x, g.reshape(1, k), w)


def _out_mlp_kernel(a_ref, wo_ref, g_mix_ref, h_ref, g_in_ref, w1_ref, w2_ref, g_out_ref, o_ref):
    h1 = h_ref[...] + _rms(_dot(a_ref[...], wo_ref[...]), g_mix_ref[...])
    xn = _rms(h1, g_in_ref[...]).astype(BF16)
    ff = w1_ref.shape[1]
    acc = None
    for c0 in range(0, ff, DOT_COLS):
        hid = jnp.maximum(_dot(xn, w1_ref[:, c0:c0 + DOT_COLS]), 0.0)
        part = _dot((hid * hid).astype(BF16), w2_ref[c0:c0 + DOT_COLS, :])
        acc = part if acc is None else acc + part
    o_ref[...] = h1 + _rms(acc, g_out_ref[...])


def _out_mlp(a, w_o, g_mix, h, g_in, w1, w2, g_out):
    m, d = h.shape
    k = a.shape[1]
    ff = w1.shape[1]
    vec = pl.BlockSpec((1, d), lambda i: (0, 0))
    resident = lambda shape: pl.BlockSpec(shape, lambda i: (0, 0), pipeline_mode=pl.Buffered(1))
    return pl.pallas_call(
        _out_mlp_kernel,
        grid=(m // MLP_ROW_TILE,),
        in_specs=[
            pl.BlockSpec((MLP_ROW_TILE, k), lambda i: (i, 0)),
            resident((k, d)),
            vec,
            pl.BlockSpec((MLP_ROW_TILE, d), lambda i: (i, 0)),
            vec,
            resident((d, ff)),
            resident((ff, d)),
            vec,
        ],
        out_specs=pl.BlockSpec((MLP_ROW_TILE, d), lambda i: (i, 0)),
        out_shape=jax.ShapeDtypeStruct((m, d), F32),
        compiler_params=pltpu.CompilerParams(
            dimension_semantics=("parallel",), vmem_limit_bytes=VMEM_LIMIT),
        name="out_mlp",
    )(a, w_o, g_mix.reshape(1, d), h, g_in.reshape(1, d), w1, w2, g_out.reshape(1, d))


def _shift_down(x, k):
    return pltpu.roll(x, k, 0)


def _shift_up(x, k):
    return pltpu.roll(x, x.shape[0] - k, 0)


def _chunk_sum_matrix():
    i = jnp.arange(MIX_ROWS)[:, None]
    j = jnp.arange(MIX_ROWS)[None, :]
    same = (i // HGRN_CHUNK) == (j // HGRN_CHUNK)
    return jnp.concatenate([same & (j <= i), same & (j > i)], axis=0).astype(BF16)


def _chunk_causal_masks():
    i = jnp.arange(MIX_ROWS)[:, None]
    j = jnp.arange(MIX_ROWS)[None, :]
    same = (i // HGRN_CHUNK) == (j // HGRN_CHUNK)
    return jnp.stack([same & (j <= i), same & (j >= i)]).astype(F32)


def _sublane_linear_scan(a, u, reverse):
    pos = lax.broadcasted_iota(jnp.int32, a.shape, 0)
    k = 1
    while k < SUBLANES:
        if reverse:
            keep = pos < SUBLANES - k
            a_n, u_n = _shift_up(a, k), _shift_up(u, k)
        else:
            keep = pos >= k
            a_n, u_n = _shift_down(a, k), _shift_down(u, k)
        u = u + jnp.where(keep, a * u_n, 0.0)
        a = a * jnp.where(keep, a_n, 1.0)
        k *= 2
    return u


def _sigmoid(x):
    return 0.5 * jnp.tanh(0.5 * x) + 0.5


def _softplus(x):
    return jnp.maximum(x, 0.0) + jnp.log1p(jnp.exp(-jnp.abs(x)))


def _gelu_tanh(x):
    return 0.5 * x * (1.0 + jnp.tanh(0.7978845608028654 * (x + 0.044715 * (x * x * x))))


def _rec_mixer_kernel(layer, seq,
                      xa_ref, ga_ref, q_ref, zf_ref, zb_ref, iv_ref, g_ref,
                      cw_ref, cb_ref, wg_ref, bg_ref, lam_ref, lbl_ref, ng_ref, tri_ref, mask_ref,
                      y_ref,
                      xpad_ref, a_ref, u_ref, pcum_ref, hloc_ref, cin_ref, o_ref, st_ref):
    n_blocks = seq // MIX_ROWS
    chunks = MIX_ROWS // HGRN_CHUNK
    assert n_blocks == SUBLANES

    logits = lbl_ref[...]
    e = jnp.exp(logits - jnp.max(logits, axis=1, keepdims=True))
    s_lb = e / jnp.sum(e, axis=1, keepdims=True)
    lb = jnp.zeros((2, LANES), F32)
    for r in range(1, layer + 1):
        lb = lb + s_lb[:, r, :]
    sp = _softplus(-lam_ref[...])

    zeros8 = jnp.zeros((SUBLANES, LANES), F32)
    xpad_ref[pl.ds(0, SUBLANES), :] = zeros8
    xpad_ref[pl.ds(seq + SUBLANES, SUBLANES), :] = zeros8
    xpad_ref[pl.ds(SUBLANES, seq), :] = xa_ref[...].astype(F32)
    st_ref[...] = jnp.zeros_like(st_ref)

    def rglru_inputs(c):
        r0 = pl.multiple_of(c * MIX_ROWS, MIX_ROWS)
        seg_rows = pl.ds(pl.multiple_of(c * SEG_PITCH, SUBLANES), MIX_ROWS)
        win = xpad_ref[pl.ds(r0, MIX_ROWS + 2 * SUBLANES), :]
        wrows = MIX_ROWS + 2 * SUBLANES
        xc = jnp.zeros((MIX_ROWS, LANES), F32) + cb_ref[...]
        for k in range(CONV_WIDTH):
            shift = (2 - k) % wrows
            shifted = (pltpu.roll(win, shift, 0) if shift else win)[SUBLANES:SUBLANES + MIX_ROWS]
            xc = xc + cw_ref[pl.ds(k, 1), :] * shifted
        gates = _dot(xc.astype(BF16), wg_ref[0]) + bg_ref[0]
        for d in range(2):
            r = _sigmoid(gates[:, (2 * d) * LANES:(2 * d + 1) * LANES])
            i = _sigmoid(gates[:, (2 * d + 1) * LANES:(2 * d + 2) * LANES])
            log_a = (-RGLRU_C) * r * sp[d:d + 1, :]
            a = jnp.exp(log_a)
            a_ref[d, seg_rows, :] = a
            u_ref[d, seg_rows, :] = jnp.sqrt(-jnp.tanh(log_a) * (1.0 + a * a)) * (i * xc)

    def vector_phase(c, carry):
        rglru_inputs(c)
        rows =[pl.ds(pl.multiple_of(blk * MIX_ROWS, MIX_ROWS), MIX_ROWS) for blk in (c, n_blocks - 1 - c)]
        log_fs, kks, parts = [], [], []
        for d, z_ref in enumerate((zf_ref, zb_ref)):
            sig = _sigmoid(z_ref[rows[d], :].astype(F32))
            lbd = lb[d:d + 1, :]
            log_f = jnp.log(lbd + (1.0 - lbd) * sig)
            log_fs.append(log_f)
            kks.append((1.0 - lbd) * (1.0 - sig))
            hi = log_f.astype(BF16)
            parts += [hi, (log_f - hi.astype(F32)).astype(BF16)]
        sums = _dot(tri_ref[...], jnp.concatenate(parts, axis=1))
        pre_f = sums[:MIX_ROWS, 0:LANES] + sums[:MIX_ROWS, LANES:2 * LANES]
        suf_f = sums[MIX_ROWS:, 0:LANES] + sums[MIX_ROWS:, LANES:2 * LANES]
        pre_b = sums[:MIX_ROWS, 2 * LANES:3 * LANES] + sums[:MIX_ROWS, 3 * LANES:]
        suf_b = sums[MIX_ROWS:, 2 * LANES:3 * LANES] + sums[MIX_ROWS:, 3 * LANES:]
        bcums = (pre_f, suf_b + log_fs[1])
        rests = (suf_f, pre_b - log_fs[1])
        chunk_slices = [slice(j * HGRN_CHUNK, (j + 1) * HGRN_CHUNK) for j in range(chunks)]
        qes, decs, o_ins, upds = [], [], [], []
        for d in range(2):
            bcum, rest, kk = bcums[d], rests[d], kks[d]
            vv = iv_ref[rows[d], :].astype(BF16)
            qe = (q_ref[rows[d], :].astype(F32) * jnp.exp(bcum)).astype(BF16)
            ke = (kk * jnp.exp(-bcum)).astype(BF16)
            kd = (kk * jnp.exp(rest)).astype(BF16)
            att = jnp.where(mask_ref[d] > 0.5, _dot_nt(qe, ke), 0.0)
            qes.append(qe)
            o_ins.append(_dot(att.astype(BF16), vv))
            decs.append([jnp.exp(bcum[ch.start:ch.start + 1, :] + rest[ch.start:ch.start + 1, :])
                         for ch in chunk_slices])
            upds.append([_dot_tn(vv[ch], kd[ch]) for ch in chunk_slices])
        entering = []
        for d in range(2):
            st = st_ref[d]
            states = [None] * chunks
            for j in (range(chunks) if d == 0 else reversed(range(chunks))):
                states[j] = st.astype(BF16)
                st = decs[d][j] * st + upds[d][j]
            st_ref[d] = st
            entering.append(states)
        for d in range(2):
            pieces = [_dot_nt(qes[d][ch], entering[d][j]) for j, ch in enumerate(chunk_slices)]
            o_ref[d, rows[d], :] = o_ins[d] + jnp.concatenate(pieces, axis=0)
        return carry

    lax.fori_loop(0, n_blocks, vector_phase, 0)

    def scan_phase(t, carry):
        hf, pf, hb, pb = carry
        tb = MIX_ROWS - 1 - t
        af = a_ref[0, pl.ds(t, SUBLANES, stride=SEG_PITCH), :]
        hf = af * hf + u_ref[0, pl.ds(t, SUBLANES, stride=SEG_PITCH), :]
        pf = pf * af
        hloc_ref[0, pl.ds(t, SUBLANES, stride=SEG_PITCH), :] = hf
        pcum_ref[0, pl.ds(t, SUBLANES, stride=SEG_PITCH), :] = pf
        ab = a_ref[1, pl.ds(tb, SUBLANES, stride=SEG_PITCH), :]
        hb = ab * hb + u_ref[1, pl.ds(tb, SUBLANES, stride=SEG_PITCH), :]
        pb = pb * ab
        hloc_ref[1, pl.ds(tb, SUBLANES, stride=SEG_PITCH), :] = hb
        pcum_ref[1, pl.ds(tb, SUBLANES, stride=SEG_PITCH), :] = pb
        return hf, pf, hb, pb

    ones8 = jnp.ones((SUBLANES, LANES), F32)
    hf, pf, hb, pb = lax.fori_loop(0, MIX_ROWS, scan_phase, (zeros8, ones8, zeros8, ones8), unroll=SCAN_UNROLL)
    sub = lax.broadcasted_iota(jnp.int32, (SUBLANES, LANES), 0)
    cin_ref[0] = jnp.where(sub >= 1, _shift_down(_sublane_linear_scan(pf, hf, False), 1), 0.0)
    cin_ref[1] = jnp.where(sub < SUBLANES - 1, _shift_up(_sublane_linear_scan(pb, hb, True), 1), 0.0)

    def out_phase(c, carry):
        rows = pl.ds(pl.multiple_of(c * MIX_ROWS, MIX_ROWS), MIX_ROWS)
        seg_rows = pl.ds(pl.multiple_of(c * SEG_PITCH, SUBLANES), MIX_ROWS)
        h = (hloc_ref[0, seg_rows, :] + pcum_ref[0, seg_rows, :] * cin_ref[0, pl.ds(c, 1), :]
             + hloc_ref[1, seg_rows, :] + pcum_ref[1, seg_rows, :] * cin_ref[1, pl.ds(c, 1), :])
        y_a = _gelu_tanh(ga_ref[rows, :].astype(F32)) * h
        gv = g_ref[rows, :].astype(F32)
        y_b = _rms(o_ref[0, rows, :] + o_ref[1, rows, :], ng_ref[...]) * (gv * _sigmoid(gv))
        y_ref[rows, pl.ds(0, LANES)] = y_a.astype(y_ref.dtype)
        y_ref[rows, pl.ds(LANES, LANES)] = y_b.astype(y_ref.dtype)
        return carry

    lax.fori_loop(0, n_blocks, out_phase, 0)


def _rec_mixer(layer, proj, batch, seq, conv_w, conv_b, w_gate, b_gate, lam, lb_logits, hg_norm):
    nh = A_HEADS
    col = lambda off: pl.BlockSpec((seq, LANES), lambda b, h, off=off: (b, off + h))
    in_specs = [col(0), col(nh), col(2 * nh), col(3 * nh), col(4 * nh), col(5 * nh), col(6 * nh),
                pl.BlockSpec((CONV_WIDTH, LANES), lambda b, h: (0, h)),
                pl.BlockSpec((1, LANES), lambda b, h: (0, h)),
                pl.BlockSpec((1, LANES, 4 * LANES), lambda b, h: (h, 0, 0)),
                pl.BlockSpec((1, 1, 4 * LANES), lambda b, h: (h, 0, 0)),
                pl.BlockSpec((2, LANES), lambda b, h: (0, h)),
                pl.BlockSpec((2, N_REC_LAYERS, LANES), lambda b, h: (0, 0, h)),
                pl.BlockSpec((1, LANES), lambda b, h: (0, h)),
                pl.BlockSpec((2 * MIX_ROWS, MIX_ROWS), lambda b, h: (0, 0)),
                pl.BlockSpec((2, MIX_ROWS, MIX_ROWS), lambda b, h: (0, 0, 0))]
    seg_total = (seq // MIX_ROWS) * SEG_PITCH
    return pl.pallas_call(
        functools.partial(_rec_mixer_kernel, layer, seq),
        grid=(batch, nh),
        in_specs=in_specs,
        out_specs=pl.BlockSpec((seq, 2 * LANES), lambda b, h: (b, h)),
        out_shape=jax.ShapeDtypeStruct((batch * seq, D_A + D_B), BF16),
        scratch_shapes=[
            pltpu.VMEM((seq + 2 * SUBLANES, LANES), F32),
            pltpu.VMEM((2, seg_total, LANES), F32),
            pltpu.VMEM((2, seg_total, LANES), F32),
            pltpu.VMEM((2, seg_total, LANES), F32),
            pltpu.VMEM((2, seg_total, LANES), F32),
            pltpu.VMEM((2, SUBLANES, LANES), F32),
            pltpu.VMEM((2, seq, LANES), F32),
            pltpu.VMEM((2, LANES, LANES), F32),
        ],
        compiler_params=pltpu.CompilerParams(
            dimension_semantics=("parallel", "parallel"), vmem_limit_bytes=VMEM_LIMIT),
        name="rec_mixer",
    )(proj, proj, proj, proj, proj, proj, proj,
      conv_w, conv_b.reshape(1, D_A), w_gate, b_gate, lam, lb_logits, hg_norm.reshape(1, D_B),
      _chunk_sum_matrix(), _chunk_causal_masks())


def _rope(t, cos, sin):
    w = t.shape[1]
    lane = lax.broadcasted_iota(jnp.int32, t.shape, 1)
    rot = jnp.where(lane % HEAD_DIM < HEAD_DIM // 2,
                    pltpu.roll(t, w - HEAD_DIM // 2, 1), pltpu.roll(t, HEAD_DIM // 2, 1))
    return t * cos + rot * sin


def _ones_row(p):
    return LANES - 1 if p == 0 else 0


def _attn_kernel(seq, q_ref, k_ref, v_ref, cos_ref, sin_ref, sink_ref, o_ref, ks_ref, vt_ref):
    n = pl.program_id(1)
    span = 3 * QBLOCK
    heads_per_tile = LANES // HEAD_DIM
    tiles_per_group = GROUP // heads_per_tile

    @pl.when(n == 0)
    def _():
        def prep(j, carry):
            r = pl.ds(pl.multiple_of(j * QBLOCK, QBLOCK), QBLOCK)
            cos_b, sin_b = cos_ref[r, :], sin_ref[r, :]
            upper = lax.broadcasted_iota(jnp.int32, (QBLOCK, LANES), 1) >= HEAD_DIM
            dim = lax.broadcasted_iota(jnp.int32, (LANES, QBLOCK), 0)
            for t in range(N_KV_HEADS // heads_per_tile):
                kt = _rope(k_ref[r, pl.ds(t * LANES, LANES)].astype(F32), cos_b, sin_b)
                vt = v_ref[r, pl.ds(t * LANES, LANES)].astype(F32).T
                for hh in range(heads_per_tile):
                    g = t * heads_per_tile + hh
                    k_same = jnp.where(upper if hh else jnp.logical_not(upper), kt, 0.0)
                    v_same = jnp.where((dim >= HEAD_DIM) if hh else (dim < HEAD_DIM), vt, 0.0)
                    k_swap = pltpu.roll(k_same, HEAD_DIM, 1)
                    v_swap = pltpu.roll(v_same, HEAD_DIM, 0)
                    k_halves = (k_same, k_swap) if hh == 0 else (k_swap, k_same)
                    v_halves = (v_same, v_swap) if hh == 0 else (v_swap, v_same)
                    for p in range(heads_per_tile):
                        ks_ref[p, g, r, :] = k_halves[p].astype(BF16)
                        vt_ref[p, g, j] = jnp.where(dim == _ones_row(p), 1.0, v_halves[p]).astype(BF16)
            return carry

        lax.fori_loop(0, seq // QBLOCK, prep, 0)

    q0 = pl.multiple_of(n * QBLOCK, QBLOCK)
    start = pl.multiple_of(jnp.clip(q0 - QBLOCK, 0, seq - span), QBLOCK)
    first_block = start // QBLOCK
    cos_q = cos_ref[pl.ds(q0, QBLOCK), :]
    sin_q = sin_ref[pl.ds(q0, QBLOCK), :]
    kpos = start + lax.broadcasted_iota(jnp.int32, (span, LANES), 0)
    qpos = q0 + lax.broadcasted_iota(jnp.int32, (span, LANES), 1)
    bias = jnp.where(jnp.abs(qpos - kpos) <= WINDOW, 0.0, -jnp.inf)
    bias = jnp.concatenate([bias] * tiles_per_group, axis=1)
    second_tile = lax.broadcasted_iota(jnp.int32, (1, tiles_per_group * LANES), 1) >= LANES
    scale = HEAD_DIM ** -0.5
    weights = {}
    for g in range(N_KV_HEADS):
        qs = jnp.concatenate(
            [(_rope(q_ref[:, pl.ds((g * tiles_per_group + t) * LANES, LANES)].astype(F32), cos_q, sin_q)
              * scale).astype(BF16)
             for t in range(tiles_per_group)], axis=0)
        for p in range(heads_per_tile):
            weights[g, p] = _dot_nt(ks_ref[p, g, pl.ds(start, span), :], qs)
    for g in range(N_KV_HEADS):
        for p in range(heads_per_tile):
            sc = weights[g, p] + bias
            sink = jnp.where(second_tile, sink_ref[g * GROUP + heads_per_tile + p], sink_ref[g * GROUP + p])
            m = jnp.maximum(jnp.max(sc, axis=0, keepdims=True), sink)
            weights[g, p] = (jnp.exp((sc - m).astype(BF16)), jnp.exp(sink - m))
    for g in range(N_KV_HEADS):
        halves = []
        for p in range(heads_per_tile):
            e, e_sink = weights[g, p]
            pv = jnp.zeros((LANES, tiles_per_group * QBLOCK), F32)
            for j in range(span // QBLOCK):
                pv = pv + _dot(vt_ref[p, g, first_block + j], e[j * QBLOCK:(j + 1) * QBLOCK, :])
            denom = pv[_ones_row(p):_ones_row(p) + 1, :] + e_sink
            halves.append(pv * (1.0 / denom))
        dim = lax.broadcasted_iota(jnp.int32, halves[0].shape, 0)
        acc = jnp.where(dim < HEAD_DIM, halves[0], halves[1])
        for t in range(tiles_per_group):
            o_ref[:, pl.ds((g * tiles_per_group + t) * LANES, LANES)] = (
                acc[:, t * QBLOCK:(t + 1) * QBLOCK].T.astype(o_ref.dtype))


def _attention(qkv, batch, seq, cos_t, sin_t, sinks):
    nb = seq // QBLOCK
    q_w = N_Q_HEADS * HEAD_DIM
    kv_w = N_KV_HEADS * HEAD_DIM
    return pl.pallas_call(
        functools.partial(_attn_kernel, seq),
        grid=(batch, nb),
        in_specs=[
            pl.BlockSpec((QBLOCK, q_w), lambda b, n: (b * nb + n, 0)),
            pl.BlockSpec((seq, kv_w), lambda b, n: (b, q_w // kv_w)),
            pl.BlockSpec((seq, kv_w), lambda b, n: (b, q_w // kv_w + 1)),
            pl.BlockSpec((seq, LANES), lambda b, n: (0, 0)),
            pl.BlockSpec((seq, LANES), lambda b, n: (0, 0)),
            pl.BlockSpec(memory_space=pltpu.SMEM),
        ],
        out_specs=pl.BlockSpec((QBLOCK, q_w), lambda b, n: (b * nb + n, 0)),
        out_shape=jax.ShapeDtypeStruct((batch * seq, q_w), BF16),
        scratch_shapes=[pltpu.VMEM((LANES // HEAD_DIM, N_KV_HEADS, seq, LANES), BF16),
                        pltpu.VMEM((LANES // HEAD_DIM, N_KV_HEADS, nb, LANES, QBLOCK), BF16)],
        compiler_params=pltpu.CompilerParams(
            dimension_semantics=("parallel", "arbitrary"), vmem_limit_bytes=VMEM_LIMIT),
        name="window_attention",
    )(qkv, qkv, qkv, cos_t, sin_t, sinks)


def _rope_tables(seq):
    pos = jnp.arange(seq, dtype=F32)
    inv_freq = ROPE_THETA ** (-jnp.arange(0, HEAD_DIM, 2, dtype=F32) / HEAD_DIM)
    ang = pos[:, None] * inv_freq[None, :]
    cos, sin = jnp.cos(ang), jnp.sin(ang)
    reps = LANES // HEAD_DIM
    return (jnp.tile(jnp.concatenate([cos, cos], axis=1), (1, reps)),
            jnp.tile(jnp.concatenate([-sin, sin], axis=1), (1, reps)))


def kernel(x, norm_g, rec_w_in, rec_conv_w, rec_conv_b, rg_w_r, rg_b_r, rg_w_i, rg_b_i, rg_lambda,
           hgrn_lb_logits, hgrn_norm_g, rec_w_out, att_w_qkv, att_sinks, att_w_o, mlp_w1, mlp_w2):
    batch, seq, d = x.shape
    cos_t, sin_t = _rope_tables(seq)
    h = x.reshape(batch * seq, d)
    w_in_all, w_qkv_all, w_att_out = rec_w_in.astype(BF16), att_w_qkv.astype(BF16), att_w_o.astype(BF16)
    w1_all, w2_all = mlp_w1.astype(BF16), mlp_w2.astype(BF16)
    w_rec_out = (rec_w_out.reshape(N_REC_LAYERS, 2, A_HEADS, A_BLOCK, d).transpose(0, 2, 1, 3, 4)
                 .reshape(N_REC_LAYERS, D_A + D_B, d).astype(BF16))
    for layer in range(DEPTH):
        g = norm_g[layer]
        if layer % 2 == 0:
            r = layer // 2
            proj = _norm_matmul(h, g[0], w_in_all[r], BF16)
            w_gate = jnp.concatenate([rg_w_r[r, 0], rg_w_i[r, 0], rg_w_r[r, 1], rg_w_i[r, 1]], axis=-1).astype(BF16)
            b_gate = jnp.stack([rg_b_r[r, 0], rg_b_i[r, 0], rg_b_r[r, 1], rg_b_i[r, 1]], axis=0)
            b_gate = b_gate.reshape(4, A_HEADS, A_BLOCK).transpose(1, 0, 2).reshape(A_HEADS, 1, 4 * A_BLOCK)
            y = _rec_mixer(r, proj, batch, seq, rec_conv_w[r], rec_conv_b[r], w_gate, b_gate, rg_lambda[r],
                           hgrn_lb_logits, hgrn_norm_g[r])
            w_out = w_rec_out[r]
        else:
            a = layer // 2
            qkv = _norm_matmul(h, g[0], w_qkv_all[a], BF16)
            y = _attention(qkv, batch, seq, cos_t, sin_t, att_sinks[a])
            w_out = w_att_out[a]
        h = _out_mlp(y, w_out, g[1], h, g[2], w1_all[layer], w2_all[layer], g[3])
    return h.reshape(batch, seq, d)
```

```python
import functools

import jax
import jax.numpy as jnp
from jax import lax
from jax.experimental import pallas as pl
from jax.experimental.pallas import tpu as pltpu

D_MODEL = 1024
DEPTH = 4
N_REC_LAYERS = (DEPTH + 1) // 2
D_A = D_MODEL // 2
A_HEADS = 4
A_BLOCK = D_A // A_HEADS
CONV_WIDTH = 4
RGLRU_C = 8.0
D_B = D_MODEL // 2
B_HEADS = 4
B_DK = D_B // B_HEADS
HGRN_CHUNK = 32
REC_IN = 2 * D_A + 5 * D_B
HEAD_DIM = 64
N_Q_HEADS = D_MODEL // HEAD_DIM
N_KV_HEADS = 4
GROUP = N_Q_HEADS // N_KV_HEADS
WINDOW = 128
QBLOCK = 128
ROPE_THETA = 10000.0
QKV_OUT = (N_Q_HEADS + 2 * N_KV_HEADS) * HEAD_DIM
D_FF = 4 * D_MODEL
EPS = 1e-6

LANES = 128
SUBLANES = 8
ROW_TILE = 1024
MLP_ROW_TILE = 512
DOT_COLS = 1024
MIX_ROWS = 256
MIX_HEADS = 2
SEG_PITCH = MIX_ROWS + SUBLANES
SCAN_UNROLL = 8
VMEM_LIMIT = 52 * 1024 * 1024

BF16 = jnp.bfloat16
F32 = jnp.float32


def _rms(x, g):
    return x * lax.rsqrt(jnp.mean(x * x, axis=-1, keepdims=True) + EPS) * g


def _dot(a, b):
    return jnp.dot(a, b, preferred_element_type=F32)


def _dot_nt(a, b):
    return lax.dot_general(a, b, (((1,), (1,)), ((), ())), preferred_element_type=F32)


def _dot_tn(a, b):
    return lax.dot_general(a, b, (((0,), (0,)), ((), ())), preferred_element_type=F32)


def _norm_matmul_kernel(x_ref, g_ref, w_ref, o_ref):
    xn = _rms(x_ref[...], g_ref[...]).astype(BF16)
    n = o_ref.shape[1]
    for c0 in range(0, n, DOT_COLS):
        c1 = min(c0 + DOT_COLS, n)
        o_ref[:, c0:c1] = _dot(xn, w_ref[:, c0:c1]).astype(o_ref.dtype)


def _norm_matmul(x, g, w_stack, idx, out_dtype):
    m, k = x.shape
    n = w_stack.shape[2]
    w = w_stack
    return pl.pallas_call(
        _norm_matmul_kernel,
        grid=(m // ROW_TILE,),
        in_specs=[
            pl.BlockSpec((ROW_TILE, k), lambda i: (i, 0)),
            pl.BlockSpec((1, k), lambda i: (0, 0)),
            pl.BlockSpec((None, k, n), lambda i: (idx, 0, 0)),
        ],
        out_specs=pl.BlockSpec((ROW_TILE, n), lambda i: (i, 0)),
        out_shape=jax.ShapeDtypeStruct((m, n), out_dtype),
        compiler_params=pltpu.CompilerParams(
            dimension_semantics=("parallel",), vmem_limit_bytes=VMEM_LIMIT),
        name="norm_matmul",
    )(x, g.reshape(1, k), w)


def _out_mlp_kernel(a_ref, wo_ref, g_mix_ref, h_ref, g_in_ref, w1_ref, w2_ref, g_out_ref, o_ref):
    h1 = h_ref[...] + _rms(_dot(a_ref[...], wo_ref[...]), g_mix_ref[...])
    xn = _rms(h1, g_in_ref[...]).astype(BF16)
    ff = w1_ref.shape[1]
    acc = None
    for c0 in range(0, ff, DOT_COLS):
        hid = jnp.maximum(_dot(xn, w1_ref[:, c0:c0 + DOT_COLS]), 0.0)
        part = _dot((hid * hid).astype(BF16), w2_ref[c0:c0 + DOT_COLS, :])
        acc = part if acc is None else acc + part
    o_ref[...] = h1 + _rms(acc, g_out_ref[...])


def _out_mlp(a, w_o, o_idx, g_mix, h, g_in, w1, w2, layer, g_out):
    m, d = h.shape
    k = a.shape[1]
    ff = w1.shape[2]
    vec = pl.BlockSpec((1, d), lambda i: (0, 0))
    resident = lambda shape, idx: pl.BlockSpec((None,) + shape, lambda i: (idx, 0, 0),
                                               pipeline_mode=pl.Buffered(1))
    return pl.pallas_call(
        _out_mlp_kernel,
        grid=(m // MLP_ROW_TILE,),
        in_specs=[
            pl.BlockSpec((MLP_ROW_TILE, k), lambda i: (i, 0)),
            resident((k, d), o_idx),
            vec,
            pl.BlockSpec((MLP_ROW_TILE, d), lambda i: (i, 0)),
            vec,
            resident((d, ff), layer),
            resident((ff, d), layer),
            vec,
        ],
        out_specs=pl.BlockSpec((MLP_ROW_TILE, d), lambda i: (i, 0)),
        out_shape=jax.ShapeDtypeStruct((m, d), F32),
        compiler_params=pltpu.CompilerParams(
            dimension_semantics=("parallel",), vmem_limit_bytes=VMEM_LIMIT),
        name="out_mlp",
    )(a, w_o, g_mix.reshape(1, d), h, g_in.reshape(1, d), w1, w2, g_out.reshape(1, d))


def _shift_down(x, k):
    return pltpu.roll(x, k, 0)


def _shift_up(x, k):
    return pltpu.roll(x, x.shape[0] - k, 0)


def _chunk_sum_matrix():
    i = jnp.arange(MIX_ROWS)[:, None]
    j = jnp.arange(MIX_ROWS)[None, :]
    same = (i // HGRN_CHUNK) == (j // HGRN_CHUNK)
    return jnp.concatenate([same & (j <= i), same & (j > i)], axis=0).astype(BF16)


def _chunk_causal_masks():
    i = jnp.arange(MIX_ROWS)[:, None]
    j = jnp.arange(MIX_ROWS)[None, :]
    same = (i // HGRN_CHUNK) == (j // HGRN_CHUNK)
    return jnp.stack([same & (j <= i), same & (j >= i)]).astype(F32)


def _sublane_linear_scan(a, u, reverse):
    pos = lax.broadcasted_iota(jnp.int32, a.shape, 0)
    k = 1
    while k < SUBLANES:
        if reverse:
            keep = pos < SUBLANES - k
            a_n, u_n = _shift_up(a, k), _shift_up(u, k)
        else:
            keep = pos >= k
            a_n, u_n = _shift_down(a, k), _shift_down(u, k)
        u = u + jnp.where(keep, a * u_n, 0.0)
        a = a * jnp.where(keep, a_n, 1.0)
        k *= 2
    return u


def _sigmoid(x):
    return 0.5 * jnp.tanh(0.5 * x) + 0.5


def _softplus(x):
    return jnp.maximum(x, 0.0) + jnp.log1p(jnp.exp(-jnp.abs(x)))


def _gelu_tanh(x):
    return 0.5 * x * (1.0 + jnp.tanh(0.7978845608028654 * (x + 0.044715 * (x * x * x))))


def _rec_mixer_kernel(layer, seq,
                      xa_ref, ga_ref, q_ref, zf_ref, zb_ref, iv_ref, g_ref,
                      cw_ref, cb_ref, wg_ref, bg_ref, lam_ref, lbl_ref, ng_ref, tri_ref, mask_ref,
                      y_ref,
                      xpad_ref, a_ref, u_ref, pcum_ref, hloc_ref, cin_ref, o_ref, st_ref):
    n_blocks = seq // MIX_ROWS
    chunks = MIX_ROWS // HGRN_CHUNK
    assert n_blocks == SUBLANES
    zeros8 = jnp.zeros((SUBLANES, LANES), F32)
    ones8 = jnp.ones((SUBLANES, LANES), F32)
    heads = range(MIX_HEADS)
    lane = [pl.ds(hd * LANES, LANES) for hd in heads]

    lbs, sps = [], []
    for hd in heads:
        logits = lbl_ref[:, :, lane[hd]]
        e = jnp.exp(logits - jnp.max(logits, axis=1, keepdims=True))
        s_lb = e / jnp.sum(e, axis=1, keepdims=True)
        lb = jnp.zeros((2, LANES), F32)
        for r in range(1, layer + 1):
            lb = lb + s_lb[:, r, :]
        lbs.append(lb)
        sps.append(_softplus(-lam_ref[:, lane[hd]]))
        xpad_ref[hd, pl.ds(0, SUBLANES), :] = zeros8
        xpad_ref[hd, pl.ds(seq + SUBLANES, SUBLANES), :] = zeros8
        xpad_ref[hd, pl.ds(SUBLANES, seq), :] = xa_ref[:, lane[hd]].astype(F32)
    st_ref[...] = jnp.zeros_like(st_ref)

    def rglru_inputs(hd, c):
        r0 = pl.multiple_of(c * MIX_ROWS, MIX_ROWS)
        seg_rows = pl.ds(pl.multiple_of(c * SEG_PITCH, SUBLANES), MIX_ROWS)
        win = xpad_ref[hd, pl.ds(r0, MIX_ROWS + 2 * SUBLANES), :]
        wrows = MIX_ROWS + 2 * SUBLANES
        xc = jnp.zeros((MIX_ROWS, LANES), F32) + cb_ref[:, lane[hd]]
        for k in range(CONV_WIDTH):
            shift = (2 - k) % wrows
            shifted = (pltpu.roll(win, shift, 0) if shift else win)[SUBLANES:SUBLANES + MIX_ROWS]
            xc = xc + cw_ref[pl.ds(k, 1), lane[hd]] * shifted
        gates = _dot(xc.astype(BF16), wg_ref[hd]) + bg_ref[hd]
        for d in range(2):
            r = _sigmoid(gates[:, (2 * d) * LANES:(2 * d + 1) * LANES])
            i = _sigmoid(gates[:, (2 * d + 1) * LANES:(2 * d + 2) * LANES])
            log_a = (-RGLRU_C) * r * sps[hd][d:d + 1, :]
            a = jnp.exp(log_a)
            a_ref[hd, d, seg_rows, :] = a
            u_ref[hd, d, seg_rows, :] = jnp.sqrt(-jnp.tanh(log_a) * (1.0 + a * a)) * (i * xc)

    def vector_phase(c, carry):
        for hd in heads:
            rglru_inputs(hd, c)
            hgrn_block(hd, c)
        return carry

    def hgrn_block(hd, c):
        lb = lbs[hd]
        rows = [pl.ds(pl.multiple_of(blk * MIX_ROWS, MIX_ROWS), MIX_ROWS) for blk in (c, n_blocks - 1 - c)]
        log_fs, kks, parts = [], [], []
        for d, z_ref in enumerate((zf_ref, zb_ref)):
            sig = _sigmoid(z_ref[rows[d], lane[hd]].astype(F32))
            lbd = lb[d:d + 1, :]
            log_f = jnp.log(lbd + (1.0 - lbd) * sig)
            log_fs.append(log_f)
            kks.append((1.0 - lbd) * (1.0 - sig))
            hi = log_f.astype(BF16)
            parts += [hi, (log_f - hi.astype(F32)).astype(BF16)]
        sums = _dot(tri_ref[...], jnp.concatenate(parts, axis=1))
        pre_f = sums[:MIX_ROWS, 0:LANES] + sums[:MIX_ROWS, LANES:2 * LANES]
        suf_f = sums[MIX_ROWS:, 0:LANES] + sums[MIX_ROWS:, LANES:2 * LANES]
        pre_b = sums[:MIX_ROWS, 2 * LANES:3 * LANES] + sums[:MIX_ROWS, 3 * LANES:]
        suf_b = sums[MIX_ROWS:, 2 * LANES:3 * LANES] + sums[MIX_ROWS:, 3 * LANES:]
        bcums = (pre_f, suf_b + log_fs[1])
        rests = (suf_f, pre_b - log_fs[1])
        chunk_slices = [slice(j * HGRN_CHUNK, (j + 1) * HGRN_CHUNK) for j in range(chunks)]
        qes, decs, o_ins, upds = [], [], [], []
        for d in range(2):
            bcum, rest, kk = bcums[d], rests[d], kks[d]
            vv = iv_ref[rows[d], lane[hd]].astype(BF16)
            qe = (q_ref[rows[d], lane[hd]].astype(F32) * jnp.exp(bcum)).astype(BF16)
            ke = (kk * jnp.exp(-bcum)).astype(BF16)
            kd = (kk * jnp.exp(rest)).astype(BF16)
            att = jnp.where(mask_ref[d] > 0.5, _dot_nt(qe, ke), 0.0)
            qes.append(qe)
            o_ins.append(_dot(att.astype(BF16), vv))
            decs.append([jnp.exp(bcum[ch.start:ch.start + 1, :] + rest[ch.start:ch.start + 1, :])
                         for ch in chunk_slices])
            upds.append([_dot_tn(vv[ch], kd[ch]) for ch in chunk_slices])
        entering = []
        for d in range(2):
            st = st_ref[hd, d]
            states = [None] * chunks
            for j in (range(chunks) if d == 0 else reversed(range(chunks))):
                states[j] = st.astype(BF16)
                st = decs[d][j] * st + upds[d][j]
            st_ref[hd, d] = st
            entering.append(states)
        for d in range(2):
            pieces = [_dot_nt(qes[d][ch], entering[d][j]) for j, ch in enumerate(chunk_slices)]
            o_ref[hd, d, rows[d], :] = o_ins[d] + jnp.concatenate(pieces, axis=0)

    lax.fori_loop(0, n_blocks, vector_phase, 0)

    def scan_phase(t, carry):
        out = []
        for hd in heads:
            hf, pf, hb, pb = carry[4 * hd:4 * hd + 4]
            tb = MIX_ROWS - 1 - t
            af = a_ref[hd, 0, pl.ds(t, SUBLANES, stride=SEG_PITCH), :]
            hf = af * hf + u_ref[hd, 0, pl.ds(t, SUBLANES, stride=SEG_PITCH), :]
            pf = pf * af
            hloc_ref[hd, 0, pl.ds(t, SUBLANES, stride=SEG_PITCH), :] = hf
            pcum_ref[hd, 0, pl.ds(t, SUBLANES, stride=SEG_PITCH), :] = pf
            ab = a_ref[hd, 1, pl.ds(tb, SUBLANES, stride=SEG_PITCH), :]
            hb = ab * hb + u_ref[hd, 1, pl.ds(tb, SUBLANES, stride=SEG_PITCH), :]
            pb = pb * ab
            hloc_ref[hd, 1, pl.ds(tb, SUBLANES, stride=SEG_PITCH), :] = hb
            pcum_ref[hd, 1, pl.ds(tb, SUBLANES, stride=SEG_PITCH), :] = pb
            out += [hf, pf, hb, pb]
        return tuple(out)

    totals = lax.fori_loop(0, MIX_ROWS, scan_phase, (zeros8, ones8, zeros8, ones8) * MIX_HEADS,
                           unroll=SCAN_UNROLL)
    sub = lax.broadcasted_iota(jnp.int32, (SUBLANES, LANES), 0)
    for hd in heads:
        hf, pf, hb, pb = totals[4 * hd:4 * hd + 4]
        cin_ref[hd, 0] = jnp.where(sub >= 1, _shift_down(_sublane_linear_scan(pf, hf, False), 1), 0.0)
        cin_ref[hd, 1] = jnp.where(sub < SUBLANES - 1, _shift_up(_sublane_linear_scan(pb, hb, True), 1), 0.0)

    def out_phase(c, carry):
        rows = pl.ds(pl.multiple_of(c * MIX_ROWS, MIX_ROWS), MIX_ROWS)
        seg_rows = pl.ds(pl.multiple_of(c * SEG_PITCH, SUBLANES), MIX_ROWS)
        for hd in heads:
            h = (hloc_ref[hd, 0, seg_rows, :] + pcum_ref[hd, 0, seg_rows, :] * cin_ref[hd, 0, pl.ds(c, 1), :]
                 + hloc_ref[hd, 1, seg_rows, :] + pcum_ref[hd, 1, seg_rows, :] * cin_ref[hd, 1, pl.ds(c, 1), :])
            y_a = _gelu_tanh(ga_ref[rows, lane[hd]].astype(F32)) * h
            gv = g_ref[rows, lane[hd]].astype(F32)
            y_b = (_rms(o_ref[hd, 0, rows, :] + o_ref[hd, 1, rows, :], ng_ref[:, lane[hd]])
                   * (gv * _sigmoid(gv)))
            y_ref[rows, pl.ds(2 * hd * LANES, LANES)] = y_a.astype(y_ref.dtype)
            y_ref[rows, pl.ds((2 * hd + 1) * LANES, LANES)] = y_b.astype(y_ref.dtype)
        return carry

    lax.fori_loop(0, n_blocks, out_phase, 0)


def _rec_mixer(layer, proj, batch, seq, conv_w, conv_b, w_gate, b_gate, lam, lb_logits, hg_norm):
    nh = A_HEADS
    hs = MIX_HEADS
    w = hs * LANES
    groups = nh // hs
    col = lambda part: pl.BlockSpec((seq, w), lambda b, h, part=part: (b, part * groups + h))
    in_specs = [col(0), col(1), col(2), col(3), col(4), col(5), col(6),
                pl.BlockSpec((CONV_WIDTH, w), lambda b, h: (0, h)),
                pl.BlockSpec((1, w), lambda b, h: (0, h)),
                pl.BlockSpec((hs, LANES, 4 * LANES), lambda b, h: (h, 0, 0)),
                pl.BlockSpec((hs, 1, 4 * LANES), lambda b, h: (h, 0, 0)),
                pl.BlockSpec((2, w), lambda b, h: (0, h)),
                pl.BlockSpec((2, N_REC_LAYERS, w), lambda b, h: (0, 0, h)),
                pl.BlockSpec((1, w), lambda b, h: (0, h)),
                pl.BlockSpec((2 * MIX_ROWS, MIX_ROWS), lambda b, h: (0, 0)),
                pl.BlockSpec((2, MIX_ROWS, MIX_ROWS), lambda b, h: (0, 0, 0))]
    seg_total = (seq // MIX_ROWS) * SEG_PITCH
    return pl.pallas_call(
        functools.partial(_rec_mixer_kernel, layer, seq),
        grid=(batch, groups),
        in_specs=in_specs,
        out_specs=pl.BlockSpec((seq, 2 * w), lambda b, h: (b, h)),
        out_shape=jax.ShapeDtypeStruct((batch * seq, D_A + D_B), BF16),
        scratch_shapes=[
            pltpu.VMEM((hs, seq + 2 * SUBLANES, LANES), F32),
            pltpu.VMEM((hs, 2, seg_total, LANES), F32),
            pltpu.VMEM((hs, 2, seg_total, LANES), F32),
            pltpu.VMEM((hs, 2, seg_total, LANES), F32),
            pltpu.VMEM((hs, 2, seg_total, LANES), F32),
            pltpu.VMEM((hs, 2, SUBLANES, LANES), F32),
            pltpu.VMEM((hs, 2, seq, LANES), F32),
            pltpu.VMEM((hs, 2, LANES, LANES), F32),
        ],
        compiler_params=pltpu.CompilerParams(
            dimension_semantics=("parallel", "parallel"), vmem_limit_bytes=VMEM_LIMIT),
        name="rec_mixer",
    )(proj, proj, proj, proj, proj, proj, proj,
      conv_w, conv_b.reshape(1, D_A), w_gate, b_gate, lam, lb_logits, hg_norm.reshape(1, D_B),
      _chunk_sum_matrix(), _chunk_causal_masks())


def _rope(t, cos, sin):
    w = t.shape[1]
    lane = lax.broadcasted_iota(jnp.int32, t.shape, 1)
    rot = jnp.where(lane % HEAD_DIM < HEAD_DIM // 2,
                    pltpu.roll(t, w - HEAD_DIM // 2, 1), pltpu.roll(t, HEAD_DIM // 2, 1))
    return t * cos + rot * sin


def _ones_row(p):
    return LANES - 1 if p == 0 else 0


def _attn_kernel(seq, q_ref, k_ref, v_ref, cos_ref, sin_ref, sink_ref, o_ref, ks_ref, vt_ref):
    n = pl.program_id(1)
    span = 3 * QBLOCK
    heads_per_tile = LANES // HEAD_DIM
    tiles_per_group = GROUP // heads_per_tile

    @pl.when(n == 0)
    def _():
        def prep(j, carry):
            r = pl.ds(pl.multiple_of(j * QBLOCK, QBLOCK), QBLOCK)
            cos_b, sin_b = cos_ref[r, :], sin_ref[r, :]
            upper = lax.broadcasted_iota(jnp.int32, (QBLOCK, LANES), 1) >= HEAD_DIM
            dim = lax.broadcasted_iota(jnp.int32, (LANES, QBLOCK), 0)
            for t in range(N_KV_HEADS // heads_per_tile):
                kt = _rope(k_ref[r, pl.ds(t * LANES, LANES)].astype(F32), cos_b, sin_b)
                vt = v_ref[r, pl.ds(t * LANES, LANES)].astype(F32).T
                for hh in range(heads_per_tile):
                    g = t * heads_per_tile + hh
                    k_same = jnp.where(upper if hh else jnp.logical_not(upper), kt, 0.0)
                    v_same = jnp.where((dim >= HEAD_DIM) if hh else (dim < HEAD_DIM), vt, 0.0)
                    k_swap = pltpu.roll(k_same, HEAD_DIM, 1)
                    v_swap = pltpu.roll(v_same, HEAD_DIM, 0)
                    k_halves = (k_same, k_swap) if hh == 0 else (k_swap, k_same)
                    v_halves = (v_same, v_swap) if hh == 0 else (v_swap, v_same)
                    for p in range(heads_per_tile):
                        ks_ref[p, g, r, :] = k_halves[p].astype(BF16)
                        vt_ref[p, g, j] = jnp.where(dim == _ones_row(p), 1.0, v_halves[p]).astype(BF16)
            return carry

        lax.fori_loop(0, seq // QBLOCK, prep, 0)

    q0 = pl.multiple_of(n * QBLOCK, QBLOCK)
    start = pl.multiple_of(jnp.clip(q0 - QBLOCK, 0, seq - span), QBLOCK)
    first_block = start // QBLOCK
    cos_q = cos_ref[pl.ds(q0, QBLOCK), :]
    sin_q = sin_ref[pl.ds(q0, QBLOCK), :]
    kpos = start + lax.broadcasted_iota(jnp.int32, (span, LANES), 0)
    qpos = q0 + lax.broadcasted_iota(jnp.int32, (span, LANES), 1)
    bias = jnp.where(jnp.abs(qpos - kpos) <= WINDOW, 0.0, -jnp.inf)
    bias = jnp.concatenate([bias] * tiles_per_group, axis=1)
    second_tile = lax.broadcasted_iota(jnp.int32, (1, tiles_per_group * LANES), 1) >= LANES
    scale = HEAD_DIM ** -0.5
    weights = {}
    for g in range(N_KV_HEADS):
        qs = jnp.concatenate(
            [(_rope(q_ref[:, pl.ds((g * tiles_per_group + t) * LANES, LANES)].astype(F32), cos_q, sin_q)
              * scale).astype(BF16)
             for t in range(tiles_per_group)], axis=0)
        for p in range(heads_per_tile):
            weights[g, p] = _dot_nt(ks_ref[p, g, pl.ds(start, span), :], qs)
    for g in range(N_KV_HEADS):
        for p in range(heads_per_tile):
            sc = weights[g, p] + bias
            sink = jnp.where(second_tile, sink_ref[g * GROUP + heads_per_tile + p], sink_ref[g * GROUP + p])
            m = jnp.maximum(jnp.max(sc, axis=0, keepdims=True), sink)
            weights[g, p] = (jnp.exp((sc - m).astype(BF16)), jnp.exp(sink - m))
    for g in range(N_KV_HEADS):
        halves = []
        for p in range(heads_per_tile):
            e, e_sink = weights[g, p]
            pv = jnp.zeros((LANES, tiles_per_group * QBLOCK), F32)
            for j in range(span // QBLOCK):
                pv = pv + _dot(vt_ref[p, g, first_block + j], e[j * QBLOCK:(j + 1) * QBLOCK, :])
            denom = pv[_ones_row(p):_ones_row(p) + 1, :] + e_sink
            halves.append(pv * (1.0 / denom))
        dim = lax.broadcasted_iota(jnp.int32, halves[0].shape, 0)
        acc = jnp.where(dim < HEAD_DIM, halves[0], halves[1])
        for t in range(tiles_per_group):
            o_ref[:, pl.ds((g * tiles_per_group + t) * LANES, LANES)] = (
                acc[:, t * QBLOCK:(t + 1) * QBLOCK].T.astype(o_ref.dtype))


def _attention(qkv, batch, seq, cos_t, sin_t, sinks):
    nb = seq // QBLOCK
    q_w = N_Q_HEADS * HEAD_DIM
    kv_w = N_KV_HEADS * HEAD_DIM
    return pl.pallas_call(
        functools.partial(_attn_kernel, seq),
        grid=(batch, nb),
        in_specs=[
            pl.BlockSpec((QBLOCK, q_w), lambda b, n: (b * nb + n, 0)),
            pl.BlockSpec((seq, kv_w), lambda b, n: (b, q_w // kv_w)),
            pl.BlockSpec((seq, kv_w), lambda b, n: (b, q_w // kv_w + 1)),
            pl.BlockSpec((seq, LANES), lambda b, n: (0, 0)),
            pl.BlockSpec((seq, LANES), lambda b, n: (0, 0)),
            pl.BlockSpec(memory_space=pltpu.SMEM),
        ],
        out_specs=pl.BlockSpec((QBLOCK, q_w), lambda b, n: (b * nb + n, 0)),
        out_shape=jax.ShapeDtypeStruct((batch * seq, q_w), BF16),
        scratch_shapes=[pltpu.VMEM((LANES // HEAD_DIM, N_KV_HEADS, seq, LANES), BF16),
                        pltpu.VMEM((LANES // HEAD_DIM, N_KV_HEADS, nb, LANES, QBLOCK), BF16)],
        compiler_params=pltpu.CompilerParams(
            dimension_semantics=("parallel", "arbitrary"), vmem_limit_bytes=VMEM_LIMIT),
        name="window_attention",
    )(qkv, qkv, qkv, cos_t, sin_t, sinks)


def _rope_tables(seq):
    pos = jnp.arange(seq, dtype=F32)
    inv_freq = ROPE_THETA ** (-jnp.arange(0, HEAD_DIM, 2, dtype=F32) / HEAD_DIM)
    ang = pos[:, None] * inv_freq[None, :]
    cos, sin = jnp.cos(ang), jnp.sin(ang)
    reps = LANES // HEAD_DIM
    return (jnp.tile(jnp.concatenate([cos, cos], axis=1), (1, reps)),
            jnp.tile(jnp.concatenate([-sin, sin], axis=1), (1, reps)))


def kernel(x, norm_g, rec_w_in, rec_conv_w, rec_conv_b, rg_w_r, rg_b_r, rg_w_i, rg_b_i, rg_lambda,
           hgrn_lb_logits, hgrn_norm_g, rec_w_out, att_w_qkv, att_sinks, att_w_o, mlp_w1, mlp_w2):
    batch, seq, d = x.shape
    cos_t, sin_t = _rope_tables(seq)
    h = x.reshape(batch * seq, d)
    w_in_all, w_qkv_all, w_att_out = rec_w_in.astype(BF16), att_w_qkv.astype(BF16), att_w_o.astype(BF16)
    w1_all, w2_all = mlp_w1.astype(BF16), mlp_w2.astype(BF16)
    w_rec_out = (rec_w_out.reshape(N_REC_LAYERS, 2, A_HEADS, A_BLOCK, d).transpose(0, 2, 1, 3, 4)
                 .reshape(N_REC_LAYERS, D_A + D_B, d).astype(BF16))
    for layer in range(DEPTH):
        g = norm_g[layer]
        if layer % 2 == 0:
            r = layer // 2
            proj = _norm_matmul(h, g[0], w_in_all, r, BF16)
            w_gate = jnp.concatenate([rg_w_r[r, 0], rg_w_i[r, 0], rg_w_r[r, 1], rg_w_i[r, 1]], axis=-1).astype(BF16)
            b_gate = jnp.stack([rg_b_r[r, 0], rg_b_i[r, 0], rg_b_r[r, 1], rg_b_i[r, 1]], axis=0)
            b_gate = b_gate.reshape(4, A_HEADS, A_BLOCK).transpose(1, 0, 2).reshape(A_HEADS, 1, 4 * A_BLOCK)
            y = _rec_mixer(r, proj, batch, seq, rec_conv_w[r], rec_conv_b[r], w_gate, b_gate, rg_lambda[r],
                           hgrn_lb_logits, hgrn_norm_g[r])
            w_out, o_idx = w_rec_out, r
        else:
            a = layer // 2
            qkv = _norm_matmul(h, g[0], w_qkv_all, a, BF16)
            y = _attention(qkv, batch, seq, cos_t, sin_t, att_sinks[a])
            w_out, o_idx = w_att_out, a
        h = _out_mlp(y, w_out, o_idx, g[1], h, g[2], w1_all, w2_all, layer, g[3])
    return h.reshape(batch, seq, d)
```

```python
import functools

import jax
import jax.numpy as jnp
from jax import lax
from jax.experimental import pallas as pl
from jax.experimental.pallas import tpu as pltpu

D_MODEL = 1024
DEPTH = 4
N_REC_LAYERS = (DEPTH + 1) // 2
D_A = D_MODEL // 2
A_HEADS = 4
A_BLOCK = D_A // A_HEADS
CONV_WIDTH = 4
RGLRU_C = 8.0
D_B = D_MODEL // 2
B_HEADS = 4
B_DK = D_B // B_HEADS
HGRN_CHUNK = 32
REC_IN = 2 * D_A + 5 * D_B
HEAD_DIM = 64
N_Q_HEADS = D_MODEL // HEAD_DIM
N_KV_HEADS = 4
GROUP = N_Q_HEADS // N_KV_HEADS
WINDOW = 128
QBLOCK = 128
ROPE_THETA = 10000.0
QKV_OUT = (N_Q_HEADS + 2 * N_KV_HEADS) * HEAD_DIM
D_FF = 4 * D_MODEL
EPS = 1e-6

LANES = 128
SUBLANES = 8
ROW_TILE = 1024
MLP_ROW_TILE = 512
DOT_COLS = 1024
MIX_ROWS = 256
MIX_HEADS = 2
SEG_PITCH = MIX_ROWS + SUBLANES
SCAN_UNROLL = 8
VMEM_LIMIT = 52 * 1024 * 1024

BF16 = jnp.bfloat16
F32 = jnp.float32


def _rms(x, g):
    return x * lax.rsqrt(jnp.mean(x * x, axis=-1, keepdims=True) + EPS) * g


def _dot(a, b):
    return jnp.dot(a, b, preferred_element_type=F32)


def _dot_nt(a, b):
    return lax.dot_general(a, b, (((1,), (1,)), ((), ())), preferred_element_type=F32)


def _dot_tn(a, b):
    return lax.dot_general(a, b, (((0,), (0,)), ((), ())), preferred_element_type=F32)


def _norm_matmul_kernel(x_ref, g_ref, w_ref, o_ref):
    xn = _rms(x_ref[...], g_ref[...]).astype(BF16)
    n = o_ref.shape[1]
    for c0 in range(0, n, DOT_COLS):
        c1 = min(c0 + DOT_COLS, n)
        o_ref[:, c0:c1] = _dot(xn, w_ref[:, c0:c1].astype(BF16)).astype(o_ref.dtype)


def _norm_matmul(x, g, w_stack, idx, out_dtype):
    m, k = x.shape
    n = w_stack.shape[2]
    w = w_stack
    return pl.pallas_call(
        _norm_matmul_kernel,
        grid=(m // ROW_TILE,),
        in_specs=[
            pl.BlockSpec((ROW_TILE, k), lambda i: (i, 0)),
            pl.BlockSpec((1, k), lambda i: (0, 0)),
            pl.BlockSpec((None, k, n), lambda i: (idx, 0, 0), pipeline_mode=pl.Buffered(1)),
        ],
        out_specs=pl.BlockSpec((ROW_TILE, n), lambda i: (i, 0)),
        out_shape=jax.ShapeDtypeStruct((m, n), out_dtype),
        compiler_params=pltpu.CompilerParams(
            dimension_semantics=("parallel",), vmem_limit_bytes=VMEM_LIMIT),
        name="norm_matmul",
    )(x, g.reshape(1, k), w)


def _out_mlp_kernel(a_ref, wo_ref, g_mix_ref, h_ref, g_in_ref, w1_ref, w2_ref, g_out_ref, o_ref):
    h1 = h_ref[...] + _rms(_dot(a_ref[...], wo_ref[...]), g_mix_ref[...])
    xn = _rms(h1, g_in_ref[...]).astype(BF16)
    ff = w1_ref.shape[1]
    acc = None
    for c0 in range(0, ff, DOT_COLS):
        hid = jnp.maximum(_dot(xn, w1_ref[:, c0:c0 + DOT_COLS].astype(BF16)), 0.0)
        part = _dot((hid * hid).astype(BF16), w2_ref[c0:c0 + DOT_COLS, :].astype(BF16))
        acc = part if acc is None else acc + part
    o_ref[...] = h1 + _rms(acc, g_out_ref[...])


def _out_mlp(a, w_o, o_idx, g_mix, h, g_in, w1, w2, layer, g_out):
    m, d = h.shape
    k = a.shape[1]
    ff = w1.shape[2]
    vec = pl.BlockSpec((1, d), lambda i: (0, 0))
    resident = lambda shape, idx: pl.BlockSpec((None,) + shape, lambda i: (idx, 0, 0),
                                               pipeline_mode=pl.Buffered(1))
    return pl.pallas_call(
        _out_mlp_kernel,
        grid=(m // MLP_ROW_TILE,),
        in_specs=[
            pl.BlockSpec((MLP_ROW_TILE, k), lambda i: (i, 0)),
            resident((k, d), o_idx),
            vec,
            pl.BlockSpec((MLP_ROW_TILE, d), lambda i: (i, 0)),
            vec,
            resident((d, ff), layer),
            resident((ff, d), layer),
            vec,
        ],
        out_specs=pl.BlockSpec((MLP_ROW_TILE, d), lambda i: (i, 0)),
        out_shape=jax.ShapeDtypeStruct((m, d), F32),
        compiler_params=pltpu.CompilerParams(
            dimension_semantics=("parallel",), vmem_limit_bytes=VMEM_LIMIT),
        name="out_mlp",
    )(a, w_o, g_mix.reshape(1, d), h, g_in.reshape(1, d), w1, w2, g_out.reshape(1, d))


def _shift_down(x, k):
    return pltpu.roll(x, k, 0)


def _shift_up(x, k):
    return pltpu.roll(x, x.shape[0] - k, 0)


def _chunk_sum_matrix():
    i = jnp.arange(MIX_ROWS)[:, None]
    j = jnp.arange(MIX_ROWS)[None, :]
    same = (i // HGRN_CHUNK) == (j // HGRN_CHUNK)
    return jnp.concatenate([same & (j <= i), same & (j > i)], axis=0).astype(BF16)


def _chunk_causal_masks():
    i = jnp.arange(MIX_ROWS)[:, None]
    j = jnp.arange(MIX_ROWS)[None, :]
    same = (i // HGRN_CHUNK) == (j // HGRN_CHUNK)
    return jnp.stack([same & (j <= i), same & (j >= i)]).astype(F32)


def _sublane_linear_scan(a, u, reverse):
    pos = lax.broadcasted_iota(jnp.int32, a.shape, 0)
    k = 1
    while k < SUBLANES:
        if reverse:
            keep = pos < SUBLANES - k
            a_n, u_n = _shift_up(a, k), _shift_up(u, k)
        else:
            keep = pos >= k
            a_n, u_n = _shift_down(a, k), _shift_down(u, k)
        u = u + jnp.where(keep, a * u_n, 0.0)
        a = a * jnp.where(keep, a_n, 1.0)
        k *= 2
    return u


def _sigmoid(x):
    return 0.5 * jnp.tanh(0.5 * x) + 0.5


def _softplus(x):
    return jnp.maximum(x, 0.0) + jnp.log1p(jnp.exp(-jnp.abs(x)))


def _gelu_tanh(x):
    return 0.5 * x * (1.0 + jnp.tanh(0.7978845608028654 * (x + 0.044715 * (x * x * x))))


def _rec_mixer_kernel(layer, seq,
                      xa_ref, ga_ref, q_ref, zf_ref, zb_ref, iv_ref, g_ref,
                      cw_ref, cb_ref, wg_ref, bg_ref, lam_ref, lbl_ref, ng_ref, tri_ref, mask_ref,
                      y_ref,
                      xpad_ref, a_ref, u_ref, pcum_ref, hloc_ref, cin_ref, o_ref, st_ref):
    n_blocks = seq // MIX_ROWS
    chunks = MIX_ROWS // HGRN_CHUNK
    assert n_blocks == SUBLANES
    zeros8 = jnp.zeros((SUBLANES, LANES), F32)
    ones8 = jnp.ones((SUBLANES, LANES), F32)
    heads = range(MIX_HEADS)
    lane = [pl.ds(hd * LANES, LANES) for hd in heads]

    lbs, sps = [], []
    for hd in heads:
        logits = lbl_ref[:, :, lane[hd]]
        e = jnp.exp(logits - jnp.max(logits, axis=1, keepdims=True))
        s_lb = e / jnp.sum(e, axis=1, keepdims=True)
        lb = jnp.zeros((2, LANES), F32)
        for r in range(1, layer + 1):
            lb = lb + s_lb[:, r, :]
        lbs.append(lb)
        sps.append(_softplus(-lam_ref[:, lane[hd]]))
        xpad_ref[hd, pl.ds(0, SUBLANES), :] = zeros8
        xpad_ref[hd, pl.ds(seq + SUBLANES, SUBLANES), :] = zeros8
        xpad_ref[hd, pl.ds(SUBLANES, seq), :] = xa_ref[:, lane[hd]].astype(F32)
    st_ref[...] = jnp.zeros_like(st_ref)

    def rglru_inputs(hd, c):
        r0 = pl.multiple_of(c * MIX_ROWS, MIX_ROWS)
        seg_rows = pl.ds(pl.multiple_of(c * SEG_PITCH, SUBLANES), MIX_ROWS)
        win = xpad_ref[hd, pl.ds(r0, MIX_ROWS + 2 * SUBLANES), :]
        wrows = MIX_ROWS + 2 * SUBLANES
        xc = jnp.zeros((MIX_ROWS, LANES), F32) + cb_ref[:, lane[hd]]
        for k in range(CONV_WIDTH):
            shift = (2 - k) % wrows
            shifted = (pltpu.roll(win, shift, 0) if shift else win)[SUBLANES:SUBLANES + MIX_ROWS]
            xc = xc + cw_ref[pl.ds(k, 1), lane[hd]] * shifted
        gates = _dot(xc.astype(BF16), wg_ref[hd]) + bg_ref[hd]
        for d in range(2):
            r = _sigmoid(gates[:, (2 * d) * LANES:(2 * d + 1) * LANES])
            i = _sigmoid(gates[:, (2 * d + 1) * LANES:(2 * d + 2) * LANES])
            log_a = (-RGLRU_C) * r * sps[hd][d:d + 1, :]
            a = jnp.exp(log_a)
            a_ref[hd, d, seg_rows, :] = a
            u_ref[hd, d, seg_rows, :] = jnp.sqrt(-jnp.tanh(log_a) * (1.0 + a * a)) * (i * xc)

    def vector_phase(c, carry):
        for hd in heads:
            rglru_inputs(hd, c)
            hgrn_block(hd, c)
        return carry

    def hgrn_block(hd, c):
        lb = lbs[hd]
        rows = [pl.ds(pl.multiple_of(blk * MIX_ROWS, MIX_ROWS), MIX_ROWS) for blk in (c, n_blocks - 1 - c)]
        log_fs, kks, parts = [], [], []
        for d, z_ref in enumerate((zf_ref, zb_ref)):
            sig = _sigmoid(z_ref[rows[d], lane[hd]].astype(F32))
            lbd = lb[d:d + 1, :]
            log_f = jnp.log(lbd + (1.0 - lbd) * sig)
            log_fs.append(log_f)
            kks.append((1.0 - lbd) * (1.0 - sig))
            hi = log_f.astype(BF16)
            parts += [hi, (log_f - hi.astype(F32)).astype(BF16)]
        sums = _dot(tri_ref[...], jnp.concatenate(parts, axis=1))
        pre_f = sums[:MIX_ROWS, 0:LANES] + sums[:MIX_ROWS, LANES:2 * LANES]
        suf_f = sums[MIX_ROWS:, 0:LANES] + sums[MIX_ROWS:, LANES:2 * LANES]
        pre_b = sums[:MIX_ROWS, 2 * LANES:3 * LANES] + sums[:MIX_ROWS, 3 * LANES:]
        suf_b = sums[MIX_ROWS:, 2 * LANES:3 * LANES] + sums[MIX_ROWS:, 3 * LANES:]
        bcums = (pre_f, suf_b + log_fs[1])
        rests = (suf_f, pre_b - log_fs[1])
        chunk_slices = [slice(j * HGRN_CHUNK, (j + 1) * HGRN_CHUNK) for j in range(chunks)]
        qes, decs, o_ins, upds = [], [], [], []
        for d in range(2):
            bcum, rest, kk = bcums[d], rests[d], kks[d]
            vv = iv_ref[rows[d], lane[hd]].astype(BF16)
            qe = (q_ref[rows[d], lane[hd]].astype(F32) * jnp.exp(bcum)).astype(BF16)
            ke = (kk * jnp.exp(-bcum)).astype(BF16)
            kd = (kk * jnp.exp(rest)).astype(BF16)
            att = jnp.where(mask_ref[d] > 0.5, _dot_nt(qe, ke), 0.0)
            qes.append(qe)
            o_ins.append(_dot(att.astype(BF16), vv))
            decs.append([jnp.exp(bcum[ch.start:ch.start + 1, :] + rest[ch.start:ch.start + 1, :])
                         for ch in chunk_slices])
            upds.append([_dot_tn(vv[ch], kd[ch]) for ch in chunk_slices])
        entering = []
        for d in range(2):
            st = st_ref[hd, d]
            states = [None] * chunks
            for j in (range(chunks) if d == 0 else reversed(range(chunks))):
                states[j] = st.astype(BF16)
                st = decs[d][j] * st + upds[d][j]
            st_ref[hd, d] = st
            entering.append(states)
        for d in range(2):
            pieces = [_dot_nt(qes[d][ch], entering[d][j]) for j, ch in enumerate(chunk_slices)]
            o_ref[hd, d, rows[d], :] = o_ins[d] + jnp.concatenate(pieces, axis=0)

    lax.fori_loop(0, n_blocks, vector_phase, 0)

    def scan_phase(t, carry):
        out = []
        for hd in heads:
            hf, pf, hb, pb = carry[4 * hd:4 * hd + 4]
            tb = MIX_ROWS - 1 - t
            af = a_ref[hd, 0, pl.ds(t, SUBLANES, stride=SEG_PITCH), :]
            hf = af * hf + u_ref[hd, 0, pl.ds(t, SUBLANES, stride=SEG_PITCH), :]
            pf = pf * af
            hloc_ref[hd, 0, pl.ds(t, SUBLANES, stride=SEG_PITCH), :] = hf
            pcum_ref[hd, 0, pl.ds(t, SUBLANES, stride=SEG_PITCH), :] = pf
            ab = a_ref[hd, 1, pl.ds(tb, SUBLANES, stride=SEG_PITCH), :]
            hb = ab * hb + u_ref[hd, 1, pl.ds(tb, SUBLANES, stride=SEG_PITCH), :]
            pb = pb * ab
            hloc_ref[hd, 1, pl.ds(tb, SUBLANES, stride=SEG_PITCH), :] = hb
            pcum_ref[hd, 1, pl.ds(tb, SUBLANES, stride=SEG_PITCH), :] = pb
            out += [hf, pf, hb, pb]
        return tuple(out)

    totals = lax.fori_loop(0, MIX_ROWS, scan_phase, (zeros8, ones8, zeros8, ones8) * MIX_HEADS,
                           unroll=SCAN_UNROLL)
    sub = lax.broadcasted_iota(jnp.int32, (SUBLANES, LANES), 0)
    for hd in heads:
        hf, pf, hb, pb = totals[4 * hd:4 * hd + 4]
        cin_ref[hd, 0] = jnp.where(sub >= 1, _shift_down(_sublane_linear_scan(pf, hf, False), 1), 0.0)
        cin_ref[hd, 1] = jnp.where(sub < SUBLANES - 1, _shift_up(_sublane_linear_scan(pb, hb, True), 1), 0.0)

    def out_phase(c, carry):
        rows = pl.ds(pl.multiple_of(c * MIX_ROWS, MIX_ROWS), MIX_ROWS)
        seg_rows = pl.ds(pl.multiple_of(c * SEG_PITCH, SUBLANES), MIX_ROWS)
        for hd in heads:
            h = (hloc_ref[hd, 0, seg_rows, :] + pcum_ref[hd, 0, seg_rows, :] * cin_ref[hd, 0, pl.ds(c, 1), :]
                 + hloc_ref[hd, 1, seg_rows, :] + pcum_ref[hd, 1, seg_rows, :] * cin_ref[hd, 1, pl.ds(c, 1), :])
            y_a = _gelu_tanh(ga_ref[rows, lane[hd]].astype(F32)) * h
            gv = g_ref[rows, lane[hd]].astype(F32)
            y_b = (_rms(o_ref[hd, 0, rows, :] + o_ref[hd, 1, rows, :], ng_ref[:, lane[hd]])
                   * (gv * _sigmoid(gv)))
            y_ref[rows, pl.ds(2 * hd * LANES, LANES)] = y_a.astype(y_ref.dtype)
            y_ref[rows, pl.ds((2 * hd + 1) * LANES, LANES)] = y_b.astype(y_ref.dtype)
        return carry

    lax.fori_loop(0, n_blocks, out_phase, 0)


def _rec_mixer(layer, proj, batch, seq, conv_w, conv_b, w_gate, b_gate, lam, lb_logits, hg_norm):
    nh = A_HEADS
    hs = MIX_HEADS
    w = hs * LANES
    groups = nh // hs
    col = lambda part: pl.BlockSpec((seq, w), lambda b, h, part=part: (b, part * groups + h))
    in_specs = [col(0), col(1), col(2), col(3), col(4), col(5), col(6),
                pl.BlockSpec((CONV_WIDTH, w), lambda b, h: (0, h)),
                pl.BlockSpec((1, w), lambda b, h: (0, h)),
                pl.BlockSpec((hs, LANES, 4 * LANES), lambda b, h: (h, 0, 0)),
                pl.BlockSpec((hs, 1, 4 * LANES), lambda b, h: (h, 0, 0)),
                pl.BlockSpec((2, w), lambda b, h: (0, h)),
                pl.BlockSpec((2, N_REC_LAYERS, w), lambda b, h: (0, 0, h)),
                pl.BlockSpec((1, w), lambda b, h: (0, h)),
                pl.BlockSpec((2 * MIX_ROWS, MIX_ROWS), lambda b, h: (0, 0)),
                pl.BlockSpec((2, MIX_ROWS, MIX_ROWS), lambda b, h: (0, 0, 0))]
    seg_total = (seq // MIX_ROWS) * SEG_PITCH
    return pl.pallas_call(
        functools.partial(_rec_mixer_kernel, layer, seq),
        grid=(batch, groups),
        in_specs=in_specs,
        out_specs=pl.BlockSpec((seq, 2 * w), lambda b, h: (b, h)),
        out_shape=jax.ShapeDtypeStruct((batch * seq, D_A + D_B), BF16),
        scratch_shapes=[
            pltpu.VMEM((hs, seq + 2 * SUBLANES, LANES), F32),
            pltpu.VMEM((hs, 2, seg_total, LANES), F32),
            pltpu.VMEM((hs, 2, seg_total, LANES), F32),
            pltpu.VMEM((hs, 2, seg_total, LANES), F32),
            pltpu.VMEM((hs, 2, seg_total, LANES), F32),
            pltpu.VMEM((hs, 2, SUBLANES, LANES), F32),
            pltpu.VMEM((hs, 2, seq, LANES), F32),
            pltpu.VMEM((hs, 2, LANES, LANES), F32),
        ],
        compiler_params=pltpu.CompilerParams(
            dimension_semantics=("parallel", "parallel"), vmem_limit_bytes=VMEM_LIMIT),
        name="rec_mixer",
    )(proj, proj, proj, proj, proj, proj, proj,
      conv_w, conv_b.reshape(1, D_A), w_gate, b_gate, lam, lb_logits, hg_norm.reshape(1, D_B),
      _chunk_sum_matrix(), _chunk_causal_masks())


def _rope(t, cos, sin):
    w = t.shape[1]
    lane = lax.broadcasted_iota(jnp.int32, t.shape, 1)
    rot = jnp.where(lane % HEAD_DIM < HEAD_DIM // 2,
                    pltpu.roll(t, w - HEAD_DIM // 2, 1), pltpu.roll(t, HEAD_DIM // 2, 1))
    return t * cos + rot * sin


PREP_UNROLL = 4
VT_PAD = 16


def _attn_kernel(seq, q_ref, k_ref, v_ref, cos_ref, sin_ref, sink_ref, o_ref, ks_ref, vt_ref):
    n = pl.program_id(1)
    span = 3 * QBLOCK
    heads_per_tile = LANES // HEAD_DIM
    tiles_per_group = GROUP // heads_per_tile

    @pl.when(n == 0)
    def _():
        def prep(j, carry):
            r = pl.ds(pl.multiple_of(j * QBLOCK, QBLOCK), QBLOCK)
            cos_b, sin_b = cos_ref[r, :], sin_ref[r, :]
            upper = lax.broadcasted_iota(jnp.int32, (QBLOCK, LANES), 1) >= HEAD_DIM
            ones_tile = jnp.where(lax.broadcasted_iota(jnp.int32, (VT_PAD, QBLOCK), 0) == 0, 1.0, 0.0)
            for t in range(N_KV_HEADS // heads_per_tile):
                kt = _rope(k_ref[r, pl.ds(t * LANES, LANES)].astype(F32), cos_b, sin_b)
                vt = v_ref[r, pl.ds(t * LANES, LANES)].astype(F32).T
                for hh in range(heads_per_tile):
                    g = t * heads_per_tile + hh
                    k_own = jnp.where(upper if hh else jnp.logical_not(upper), kt, 0.0)
                    ks_ref[g, r, :] = (k_own + pltpu.roll(k_own, HEAD_DIM, 1)).astype(BF16)
                    vt_ref[g, j, pl.ds(0, HEAD_DIM), :] = vt[hh * HEAD_DIM:(hh + 1) * HEAD_DIM].astype(BF16)
                    vt_ref[g, j, pl.ds(HEAD_DIM, VT_PAD), :] = ones_tile.astype(BF16)
            return carry

        lax.fori_loop(0, seq // QBLOCK, prep, 0, unroll=PREP_UNROLL)

    q0 = pl.multiple_of(n * QBLOCK, QBLOCK)
    start = pl.multiple_of(jnp.clip(q0 - QBLOCK, 0, seq - span), QBLOCK)
    first_block = start // QBLOCK
    cos_q = cos_ref[pl.ds(q0, QBLOCK), :]
    sin_q = sin_ref[pl.ds(q0, QBLOCK), :]
    kpos = start + lax.broadcasted_iota(jnp.int32, (span, LANES), 0)
    qpos = q0 + lax.broadcasted_iota(jnp.int32, (span, LANES), 1)
    bias = jnp.where(jnp.abs(qpos - kpos) <= WINDOW, 0.0, -jnp.inf)
    cols = GROUP * QBLOCK
    bias = jnp.concatenate([bias] * GROUP, axis=1)
    col = lax.broadcasted_iota(jnp.int32, (1, cols), 1)
    upper = lax.broadcasted_iota(jnp.int32, (tiles_per_group * QBLOCK, LANES), 1) >= HEAD_DIM
    scale = HEAD_DIM ** -0.5
    scores, weights = [], []
    for g in range(N_KV_HEADS):
        qs = jnp.concatenate(
            [(_rope(q_ref[:, pl.ds((g * tiles_per_group + t) * LANES, LANES)].astype(F32), cos_q, sin_q)
              * scale).astype(BF16)
             for t in range(tiles_per_group)], axis=0)
        zero = jnp.zeros_like(qs)
        rhs = jnp.concatenate([jnp.where(upper, zero, qs), jnp.where(upper, qs, zero)], axis=0)
        scores.append(_dot_nt(ks_ref[g, pl.ds(start, span), :], rhs))
    for g in range(N_KV_HEADS):
        sc = scores[g] + bias
        sink = sink_ref[g * GROUP]
        for p in range(heads_per_tile):
            for t in range(tiles_per_group):
                if p or t:
                    sink = jnp.where(col >= (p * tiles_per_group + t) * QBLOCK,
                                     sink_ref[g * GROUP + t * heads_per_tile + p], sink)
        m = jnp.maximum(jnp.max(sc, axis=0, keepdims=True), sink)
        weights.append((jnp.exp((sc - m).astype(BF16)), jnp.exp(sink - m)))
    for g in range(N_KV_HEADS):
        e, e_sink = weights[g]
        pv = jnp.zeros((HEAD_DIM + VT_PAD, cols), F32)
        for j in range(span // QBLOCK):
            pv = pv + _dot(vt_ref[g, first_block + j], e[j * QBLOCK:(j + 1) * QBLOCK, :])
        out = pv[:HEAD_DIM] * (1.0 / (pv[HEAD_DIM:HEAD_DIM + 1] + e_sink))
        for t in range(tiles_per_group):
            tile = jnp.concatenate(
                [out[:, (p * tiles_per_group + t) * QBLOCK:(p * tiles_per_group + t + 1) * QBLOCK]
                 for p in range(heads_per_tile)], axis=0)
            o_ref[:, pl.ds((g * tiles_per_group + t) * LANES, LANES)] = tile.T.astype(o_ref.dtype)


def _attention(qkv, batch, seq, cos_t, sin_t, sinks):
    nb = seq // QBLOCK
    q_w = N_Q_HEADS * HEAD_DIM
    kv_w = N_KV_HEADS * HEAD_DIM
    return pl.pallas_call(
        functools.partial(_attn_kernel, seq),
        grid=(batch, nb),
        in_specs=[
            pl.BlockSpec((QBLOCK, q_w), lambda b, n: (b * nb + n, 0)),
            pl.BlockSpec((seq, kv_w), lambda b, n: (b, q_w // kv_w)),
            pl.BlockSpec((seq, kv_w), lambda b, n: (b, q_w // kv_w + 1)),
            pl.BlockSpec((seq, LANES), lambda b, n: (0, 0)),
            pl.BlockSpec((seq, LANES), lambda b, n: (0, 0)),
            pl.BlockSpec(memory_space=pltpu.SMEM),
        ],
        out_specs=pl.BlockSpec((QBLOCK, q_w), lambda b, n: (b * nb + n, 0)),
        out_shape=jax.ShapeDtypeStruct((batch * seq, q_w), BF16),
        scratch_shapes=[pltpu.VMEM((N_KV_HEADS, seq, LANES), BF16),
                        pltpu.VMEM((N_KV_HEADS, nb, HEAD_DIM + VT_PAD, QBLOCK), BF16)],
        compiler_params=pltpu.CompilerParams(
            dimension_semantics=("parallel", "arbitrary"), vmem_limit_bytes=VMEM_LIMIT),
        name="window_attention",
    )(qkv, qkv, qkv, cos_t, sin_t, sinks)


def _rope_tables(seq):
    pos = jnp.arange(seq, dtype=F32)
    inv_freq = ROPE_THETA ** (-jnp.arange(0, HEAD_DIM, 2, dtype=F32) / HEAD_DIM)
    ang = pos[:, None] * inv_freq[None, :]
    cos, sin = jnp.cos(ang), jnp.sin(ang)
    reps = LANES // HEAD_DIM
    return (jnp.tile(jnp.concatenate([cos, cos], axis=1), (1, reps)),
            jnp.tile(jnp.concatenate([-sin, sin], axis=1), (1, reps)))


def kernel(x, norm_g, rec_w_in, rec_conv_w, rec_conv_b, rg_w_r, rg_b_r, rg_w_i, rg_b_i, rg_lambda,
           hgrn_lb_logits, hgrn_norm_g, rec_w_out, att_w_qkv, att_sinks, att_w_o, mlp_w1, mlp_w2):
    batch, seq, d = x.shape
    cos_t, sin_t = _rope_tables(seq)
    h = x.reshape(batch * seq, d)
    w_att_out = att_w_o.astype(BF16)
    w_in_all, w_qkv_all = rec_w_in, att_w_qkv
    w1_all, w2_all = mlp_w1, mlp_w2
    w_rec_out = (rec_w_out.reshape(N_REC_LAYERS, 2, A_HEADS, A_BLOCK, d).transpose(0, 2, 1, 3, 4)
                 .reshape(N_REC_LAYERS, D_A + D_B, d).astype(BF16))
    for layer in range(DEPTH):
        g = norm_g[layer]
        if layer % 2 == 0:
            r = layer // 2
            proj = _norm_matmul(h, g[0], w_in_all, r, BF16)
            w_gate = jnp.concatenate([rg_w_r[r, 0], rg_w_i[r, 0], rg_w_r[r, 1], rg_w_i[r, 1]], axis=-1).astype(BF16)
            b_gate = jnp.stack([rg_b_r[r, 0], rg_b_i[r, 0], rg_b_r[r, 1], rg_b_i[r, 1]], axis=0)
            b_gate = b_gate.reshape(4, A_HEADS, A_BLOCK).transpose(1, 0, 2).reshape(A_HEADS, 1, 4 * A_BLOCK)
            y = _rec_mixer(r, proj, batch, seq, rec_conv_w[r], rec_conv_b[r], w_gate, b_gate, rg_lambda[r],
                           hgrn_lb_logits, hgrn_norm_g[r])
            w_out, o_idx = w_rec_out, r
        else:
            a = layer // 2
            qkv = _norm_matmul(h, g[0], w_qkv_all, a, BF16)
            y = _attention(qkv, batch, seq, cos_t, sin_t, att_sinks[a])
            w_out, o_idx = w_att_out, a
        h = _out_mlp(y, w_out, o_idx, g[1], h, g[2], w1_all, w2_all, layer, g[3])
    return h.reshape(batch, seq, d)
```

```python
import functools

import jax
import jax.numpy as jnp
from jax import lax
from jax.experimental import pallas as pl
from jax.experimental.pallas import tpu as pltpu

D_MODEL = 1024
DEPTH = 4
N_REC_LAYERS = (DEPTH + 1) // 2
D_A = D_MODEL // 2
A_HEADS = 4
A_BLOCK = D_A // A_HEADS
CONV_WIDTH = 4
RGLRU_C = 8.0
D_B = D_MODEL // 2
B_HEADS = 4
B_DK = D_B // B_HEADS
HGRN_CHUNK = 32
REC_IN = 2 * D_A + 5 * D_B
HEAD_DIM = 64
N_Q_HEADS = D_MODEL // HEAD_DIM
N_KV_HEADS = 4
GROUP = N_Q_HEADS // N_KV_HEADS
WINDOW = 128
QBLOCK = 128
ROPE_THETA = 10000.0
QKV_OUT = (N_Q_HEADS + 2 * N_KV_HEADS) * HEAD_DIM
D_FF = 4 * D_MODEL
EPS = 1e-6

LANES = 128
SUBLANES = 8
ROW_TILE = 1024
MLP_ROW_TILE = 512
DOT_COLS = 1024
MIX_ROWS = 256
MIX_HEADS = 2
SEG_PITCH = MIX_ROWS + SUBLANES
SCAN_UNROLL = 8
VMEM_LIMIT = 52 * 1024 * 1024

BF16 = jnp.bfloat16
F32 = jnp.float32


def _rms(x, g):
    return x * lax.rsqrt(jnp.mean(x * x, axis=-1, keepdims=True) + EPS) * g


def _dot(a, b):
    return jnp.dot(a, b, preferred_element_type=F32)


def _dot_nt(a, b):
    return lax.dot_general(a, b, (((1,), (1,)), ((), ())), preferred_element_type=F32)


def _dot_tn(a, b):
    return lax.dot_general(a, b, (((0,), (0,)), ((), ())), preferred_element_type=F32)


def _norm_matmul_kernel(x_ref, g_ref, w_ref, o_ref):
    xn = _rms(x_ref[...], g_ref[...]).astype(BF16)
    n = o_ref.shape[1]
    for c0 in range(0, n, DOT_COLS):
        c1 = min(c0 + DOT_COLS, n)
        o_ref[:, c0:c1] = _dot(xn, w_ref[:, c0:c1].astype(BF16)).astype(o_ref.dtype)


def _rope(t, cos, sin):
    w = t.shape[1]
    lane = lax.broadcasted_iota(jnp.int32, t.shape, 1)
    rot = jnp.where(lane % HEAD_DIM < HEAD_DIM // 2,
                    pltpu.roll(t, w - HEAD_DIM // 2, 1), pltpu.roll(t, HEAD_DIM // 2, 1))
    return t * cos + rot * sin


def _qkv_proj_kernel(x_ref, g_ref, w_ref, cos_ref, sin_ref, o_ref):
    xn = _rms(x_ref[...], g_ref[...]).astype(BF16)
    q_w = N_Q_HEADS * HEAD_DIM
    rope_w = q_w + N_KV_HEADS * HEAD_DIM
    cos, sin = cos_ref[...], sin_ref[...]
    n = o_ref.shape[1]
    for c0 in range(0, n, DOT_COLS):
        c1 = min(c0 + DOT_COLS, n)
        res = _dot(xn, w_ref[:, c0:c1].astype(BF16))
        for t0 in range(c0, c1, LANES):
            tile = res[:, t0 - c0:t0 - c0 + LANES]
            if t0 < rope_w:
                tile = _rope(tile, cos, sin)
            if t0 < q_w:
                tile = tile * HEAD_DIM ** -0.5
            o_ref[:, t0:t0 + LANES] = tile.astype(o_ref.dtype)


def _qkv_proj(x, g, w_stack, idx, cos_t, sin_t, seq):
    m, k = x.shape
    n = w_stack.shape[2]
    tiles_per_seq = seq // ROW_TILE
    table = pl.BlockSpec((ROW_TILE, LANES), lambda i: (i % tiles_per_seq, 0))
    return pl.pallas_call(
        _qkv_proj_kernel,
        grid=(m // ROW_TILE,),
        in_specs=[
            pl.BlockSpec((ROW_TILE, k), lambda i: (i, 0)),
            pl.BlockSpec((1, k), lambda i: (0, 0)),
            pl.BlockSpec((None, k, n), lambda i: (idx, 0, 0), pipeline_mode=pl.Buffered(1)),
            table, table,
        ],
        out_specs=pl.BlockSpec((ROW_TILE, n), lambda i: (i, 0)),
        out_shape=jax.ShapeDtypeStruct((m, n), BF16),
        compiler_params=pltpu.CompilerParams(
            dimension_semantics=("parallel",), vmem_limit_bytes=VMEM_LIMIT),
        name="qkv_proj",
    )(x, g.reshape(1, k), w_stack, cos_t, sin_t)


def _norm_matmul(x, g, w_stack, idx, out_dtype):
    m, k = x.shape
    n = w_stack.shape[2]
    w = w_stack
    return pl.pallas_call(
        _norm_matmul_kernel,
        grid=(m // ROW_TILE,),
        in_specs=[
            pl.BlockSpec((ROW_TILE, k), lambda i: (i, 0)),
            pl.BlockSpec((1, k), lambda i: (0, 0)),
            pl.BlockSpec((None, k, n), lambda i: (idx, 0, 0), pipeline_mode=pl.Buffered(1)),
        ],
        out_specs=pl.BlockSpec((ROW_TILE, n), lambda i: (i, 0)),
        out_shape=jax.ShapeDtypeStruct((m, n), out_dtype),
        compiler_params=pltpu.CompilerParams(
            dimension_semantics=("parallel",), vmem_limit_bytes=VMEM_LIMIT),
        name="norm_matmul",
    )(x, g.reshape(1, k), w)


def _out_mlp_kernel(a_ref, wo_ref, g_mix_ref, h_ref, g_in_ref, w1_ref, w2_ref, g_out_ref, o_ref):
    h1 = h_ref[...] + _rms(_dot(a_ref[...], wo_ref[...]), g_mix_ref[...])
    xn = _rms(h1, g_in_ref[...]).astype(BF16)
    ff = w1_ref.shape[1]
    acc = None
    for c0 in range(0, ff, DOT_COLS):
        hid = jnp.maximum(_dot(xn, w1_ref[:, c0:c0 + DOT_COLS].astype(BF16)), 0.0)
        part = _dot((hid * hid).astype(BF16), w2_ref[c0:c0 + DOT_COLS, :].astype(BF16))
        acc = part if acc is None else acc + part
    o_ref[...] = h1 + _rms(acc, g_out_ref[...])


def _out_mlp(a, w_o, o_idx, g_mix, h, g_in, w1, w2, layer, g_out):
    m, d = h.shape
    k = a.shape[1]
    ff = w1.shape[2]
    vec = pl.BlockSpec((1, d), lambda i: (0, 0))
    resident = lambda shape, idx: pl.BlockSpec((None,) + shape, lambda i: (idx, 0, 0),
                                               pipeline_mode=pl.Buffered(1))
    return pl.pallas_call(
        _out_mlp_kernel,
        grid=(m // MLP_ROW_TILE,),
        in_specs=[
            pl.BlockSpec((MLP_ROW_TILE, k), lambda i: (i, 0)),
            resident((k, d), o_idx),
            vec,
            pl.BlockSpec((MLP_ROW_TILE, d), lambda i: (i, 0)),
            vec,
            resident((d, ff), layer),
            resident((ff, d), layer),
            vec,
        ],
        out_specs=pl.BlockSpec((MLP_ROW_TILE, d), lambda i: (i, 0)),
        out_shape=jax.ShapeDtypeStruct((m, d), F32),
        compiler_params=pltpu.CompilerParams(
            dimension_semantics=("parallel",), vmem_limit_bytes=VMEM_LIMIT),
        name="out_mlp",
    )(a, w_o, g_mix.reshape(1, d), h, g_in.reshape(1, d), w1, w2, g_out.reshape(1, d))


def _shift_down(x, k):
    return pltpu.roll(x, k, 0)


def _shift_up(x, k):
    return pltpu.roll(x, x.shape[0] - k, 0)


def _chunk_sum_matrix():
    i = jnp.arange(MIX_ROWS)[:, None]
    j = jnp.arange(MIX_ROWS)[None, :]
    same = (i // HGRN_CHUNK) == (j // HGRN_CHUNK)
    return jnp.concatenate([same & (j <= i), same & (j > i)], axis=0).astype(BF16)


def _chunk_causal_masks():
    i = jnp.arange(MIX_ROWS)[:, None]
    j = jnp.arange(MIX_ROWS)[None, :]
    same = (i // HGRN_CHUNK) == (j // HGRN_CHUNK)
    return jnp.stack([same & (j <= i), same & (j >= i)]).astype(F32)


def _sublane_linear_scan(a, u, reverse):
    pos = lax.broadcasted_iota(jnp.int32, a.shape, 0)
    k = 1
    while k < SUBLANES:
        if reverse:
            keep = pos < SUBLANES - k
            a_n, u_n = _shift_up(a, k), _shift_up(u, k)
        else:
            keep = pos >= k
            a_n, u_n = _shift_down(a, k), _shift_down(u, k)
        u = u + jnp.where(keep, a * u_n, 0.0)
        a = a * jnp.where(keep, a_n, 1.0)
        k *= 2
    return u


def _sigmoid(x):
    return 0.5 * jnp.tanh(0.5 * x) + 0.5


def _softplus(x):
    return jnp.maximum(x, 0.0) + jnp.log1p(jnp.exp(-jnp.abs(x)))


def _gelu_tanh(x):
    return 0.5 * x * (1.0 + jnp.tanh(0.7978845608028654 * (x + 0.044715 * (x * x * x))))


def _rec_mixer_kernel(layer, seq,
                      xa_ref, ga_ref, q_ref, zf_ref, zb_ref, iv_ref, g_ref,
                      cw_ref, cb_ref, wg_ref, bg_ref, lam_ref, lbl_ref, ng_ref, tri_ref, mask_ref,
                      y_ref,
                      xpad_ref, a_ref, u_ref, pcum_ref, hloc_ref, cin_ref, o_ref, st_ref):
    n_blocks = seq // MIX_ROWS
    chunks = MIX_ROWS // HGRN_CHUNK
    assert n_blocks == SUBLANES
    zeros8 = jnp.zeros((SUBLANES, LANES), F32)
    ones8 = jnp.ones((SUBLANES, LANES), F32)
    heads = range(MIX_HEADS)
    lane = [pl.ds(hd * LANES, LANES) for hd in heads]

    lbs, sps = [], []
    for hd in heads:
        logits = lbl_ref[:, :, lane[hd]]
        e = jnp.exp(logits - jnp.max(logits, axis=1, keepdims=True))
        s_lb = e / jnp.sum(e, axis=1, keepdims=True)
        lb = jnp.zeros((2, LANES), F32)
        for r in range(1, layer + 1):
            lb = lb + s_lb[:, r, :]
        lbs.append(lb)
        sps.append(_softplus(-lam_ref[:, lane[hd]]))
        xpad_ref[hd, pl.ds(0, SUBLANES), :] = zeros8
        xpad_ref[hd, pl.ds(seq + SUBLANES, SUBLANES), :] = zeros8
        xpad_ref[hd, pl.ds(SUBLANES, seq), :] = xa_ref[:, lane[hd]].astype(F32)
    st_ref[...] = jnp.zeros_like(st_ref)

    def rglru_inputs(hd, c):
        r0 = pl.multiple_of(c * MIX_ROWS, MIX_ROWS)
        seg_rows = pl.ds(pl.multiple_of(c * SEG_PITCH, SUBLANES), MIX_ROWS)
        win = xpad_ref[hd, pl.ds(r0, MIX_ROWS + 2 * SUBLANES), :]
        wrows = MIX_ROWS + 2 * SUBLANES
        xc = jnp.zeros((MIX_ROWS, LANES), F32) + cb_ref[:, lane[hd]]
        for k in range(CONV_WIDTH):
            shift = (2 - k) % wrows
            shifted = (pltpu.roll(win, shift, 0) if shift else win)[SUBLANES:SUBLANES + MIX_ROWS]
            xc = xc + cw_ref[pl.ds(k, 1), lane[hd]] * shifted
        gates = _dot(xc.astype(BF16), wg_ref[hd]) + bg_ref[hd]
        for d in range(2):
            r = _sigmoid(gates[:, (2 * d) * LANES:(2 * d + 1) * LANES])
            i = _sigmoid(gates[:, (2 * d + 1) * LANES:(2 * d + 2) * LANES])
            log_a = (-RGLRU_C) * r * sps[hd][d:d + 1, :]
            a = jnp.exp(log_a)
            a_ref[hd, d, seg_rows, :] = a
            u_ref[hd, d, seg_rows, :] = jnp.sqrt(-jnp.tanh(log_a) * (1.0 + a * a)) * (i * xc)

    def vector_phase(c, carry):
        for hd in heads:
            rglru_inputs(hd, c)
            hgrn_block(hd, c)
        return carry

    def hgrn_block(hd, c):
        lb = lbs[hd]
        rows = [pl.ds(pl.multiple_of(blk * MIX_ROWS, MIX_ROWS), MIX_ROWS) for blk in (c, n_blocks - 1 - c)]
        log_fs, kks, parts = [], [], []
        for d, z_ref in enumerate((zf_ref, zb_ref)):
            sig = _sigmoid(z_ref[rows[d], lane[hd]].astype(F32))
            lbd = lb[d:d + 1, :]
            log_f = jnp.log(lbd + (1.0 - lbd) * sig)
            log_fs.append(log_f)
            kks.append((1.0 - lbd) * (1.0 - sig))
            hi = log_f.astype(BF16)
            parts += [hi, (log_f - hi.astype(F32)).astype(BF16)]
        sums = _dot(tri_ref[...], jnp.concatenate(parts, axis=1))
        pre_f = sums[:MIX_ROWS, 0:LANES] + sums[:MIX_ROWS, LANES:2 * LANES]
        suf_f = sums[MIX_ROWS:, 0:LANES] + sums[MIX_ROWS:, LANES:2 * LANES]
        pre_b = sums[:MIX_ROWS, 2 * LANES:3 * LANES] + sums[:MIX_ROWS, 3 * LANES:]
        suf_b = sums[MIX_ROWS:, 2 * LANES:3 * LANES] + sums[MIX_ROWS:, 3 * LANES:]
        bcums = (pre_f, suf_b + log_fs[1])
        rests = (suf_f, pre_b - log_fs[1])
        chunk_slices = [slice(j * HGRN_CHUNK, (j + 1) * HGRN_CHUNK) for j in range(chunks)]
        qes, decs, o_ins, upds = [], [], [], []
        for d in range(2):
            bcum, rest, kk = bcums[d], rests[d], kks[d]
            vv = iv_ref[rows[d], lane[hd]].astype(BF16)
            qe = (q_ref[rows[d], lane[hd]].astype(F32) * jnp.exp(bcum)).astype(BF16)
            ke = (kk * jnp.exp(-bcum)).astype(BF16)
            kd = (kk * jnp.exp(rest)).astype(BF16)
            att = jnp.where(mask_ref[d] > 0.5, _dot_nt(qe, ke), 0.0)
            qes.append(qe)
            o_ins.append(_dot(att.astype(BF16), vv))
            decs.append([jnp.exp(bcum[ch.start:ch.start + 1, :] + rest[ch.start:ch.start + 1, :])
                         for ch in chunk_slices])
            upds.append([_dot_tn(vv[ch], kd[ch]) for ch in chunk_slices])
        entering = []
        for d in range(2):
            st = st_ref[hd, d]
            states = [None] * chunks
            for j in (range(chunks) if d == 0 else reversed(range(chunks))):
                states[j] = st.astype(BF16)
                st = decs[d][j] * st + upds[d][j]
            st_ref[hd, d] = st
            entering.append(states)
        for d in range(2):
            pieces = [_dot_nt(qes[d][ch], entering[d][j]) for j, ch in enumerate(chunk_slices)]
            o_ref[hd, d, rows[d], :] = o_ins[d] + jnp.concatenate(pieces, axis=0)

    lax.fori_loop(0, n_blocks, vector_phase, 0)

    def scan_phase(t, carry):
        out = []
        for hd in heads:
            hf, pf, hb, pb = carry[4 * hd:4 * hd + 4]
            tb = MIX_ROWS - 1 - t
            af = a_ref[hd, 0, pl.ds(t, SUBLANES, stride=SEG_PITCH), :]
            hf = af * hf + u_ref[hd, 0, pl.ds(t, SUBLANES, stride=SEG_PITCH), :]
            pf = pf * af
            hloc_ref[hd, 0, pl.ds(t, SUBLANES, stride=SEG_PITCH), :] = hf
            pcum_ref[hd, 0, pl.ds(t, SUBLANES, stride=SEG_PITCH), :] = pf
            ab = a_ref[hd, 1, pl.ds(tb, SUBLANES, stride=SEG_PITCH), :]
            hb = ab * hb + u_ref[hd, 1, pl.ds(tb, SUBLANES, stride=SEG_PITCH), :]
            pb = pb * ab
            hloc_ref[hd, 1, pl.ds(tb, SUBLANES, stride=SEG_PITCH), :] = hb
            pcum_ref[hd, 1, pl.ds(tb, SUBLANES, stride=SEG_PITCH), :] = pb
            out += [hf, pf, hb, pb]
        return tuple(out)

    totals = lax.fori_loop(0, MIX_ROWS, scan_phase, (zeros8, ones8, zeros8, ones8) * MIX_HEADS,
                           unroll=SCAN_UNROLL)
    sub = lax.broadcasted_iota(jnp.int32, (SUBLANES, LANES), 0)
    for hd in heads:
        hf, pf, hb, pb = totals[4 * hd:4 * hd + 4]
        cin_ref[hd, 0] = jnp.where(sub >= 1, _shift_down(_sublane_linear_scan(pf, hf, False), 1), 0.0)
        cin_ref[hd, 1] = jnp.where(sub < SUBLANES - 1, _shift_up(_sublane_linear_scan(pb, hb, True), 1), 0.0)

    def out_phase(c, carry):
        rows = pl.ds(pl.multiple_of(c * MIX_ROWS, MIX_ROWS), MIX_ROWS)
        seg_rows = pl.ds(pl.multiple_of(c * SEG_PITCH, SUBLANES), MIX_ROWS)
        for hd in heads:
            h = (hloc_ref[hd, 0, seg_rows, :] + pcum_ref[hd, 0, seg_rows, :] * cin_ref[hd, 0, pl.ds(c, 1), :]
                 + hloc_ref[hd, 1, seg_rows, :] + pcum_ref[hd, 1, seg_rows, :] * cin_ref[hd, 1, pl.ds(c, 1), :])
            y_a = _gelu_tanh(ga_ref[rows, lane[hd]].astype(F32)) * h
            gv = g_ref[rows, lane[hd]].astype(F32)
            y_b = (_rms(o_ref[hd, 0, rows, :] + o_ref[hd, 1, rows, :], ng_ref[:, lane[hd]])
                   * (gv * _sigmoid(gv)))
            y_ref[rows, pl.ds(2 * hd * LANES, LANES)] = y_a.astype(y_ref.dtype)
            y_ref[rows, pl.ds((2 * hd + 1) * LANES, LANES)] = y_b.astype(y_ref.dtype)
        return carry

    lax.fori_loop(0, n_blocks, out_phase, 0)


def _rec_mixer(layer, proj, batch, seq, conv_w, conv_b, w_gate, b_gate, lam, lb_logits, hg_norm):
    nh = A_HEADS
    hs = MIX_HEADS
    w = hs * LANES
    groups = nh // hs
    col = lambda part: pl.BlockSpec((seq, w), lambda b, h, part=part: (b, part * groups + h))
    in_specs = [col(0), col(1), col(2), col(3), col(4), col(5), col(6),
                pl.BlockSpec((CONV_WIDTH, w), lambda b, h: (0, h)),
                pl.BlockSpec((1, w), lambda b, h: (0, h)),
                pl.BlockSpec((hs, LANES, 4 * LANES), lambda b, h: (h, 0, 0)),
                pl.BlockSpec((hs, 1, 4 * LANES), lambda b, h: (h, 0, 0)),
                pl.BlockSpec((2, w), lambda b, h: (0, h)),
                pl.BlockSpec((2, N_REC_LAYERS, w), lambda b, h: (0, 0, h)),
                pl.BlockSpec((1, w), lambda b, h: (0, h)),
                pl.BlockSpec((2 * MIX_ROWS, MIX_ROWS), lambda b, h: (0, 0)),
                pl.BlockSpec((2, MIX_ROWS, MIX_ROWS), lambda b, h: (0, 0, 0))]
    seg_total = (seq // MIX_ROWS) * SEG_PITCH
    return pl.pallas_call(
        functools.partial(_rec_mixer_kernel, layer, seq),
        grid=(batch, groups),
        in_specs=in_specs,
        out_specs=pl.BlockSpec((seq, 2 * w), lambda b, h: (b, h)),
        out_shape=jax.ShapeDtypeStruct((batch * seq, D_A + D_B), BF16),
        scratch_shapes=[
            pltpu.VMEM((hs, seq + 2 * SUBLANES, LANES), F32),
            pltpu.VMEM((hs, 2, seg_total, LANES), F32),
            pltpu.VMEM((hs, 2, seg_total, LANES), F32),
            pltpu.VMEM((hs, 2, seg_total, LANES), F32),
            pltpu.VMEM((hs, 2, seg_total, LANES), F32),
            pltpu.VMEM((hs, 2, SUBLANES, LANES), F32),
            pltpu.VMEM((hs, 2, seq, LANES), F32),
            pltpu.VMEM((hs, 2, LANES, LANES), F32),
        ],
        compiler_params=pltpu.CompilerParams(
            dimension_semantics=("parallel", "parallel"), vmem_limit_bytes=VMEM_LIMIT),
        name="rec_mixer",
    )(proj, proj, proj, proj, proj, proj, proj,
      conv_w, conv_b.reshape(1, D_A), w_gate, b_gate, lam, lb_logits, hg_norm.reshape(1, D_B),
      _chunk_sum_matrix(), _chunk_causal_masks())


PREP_UNROLL = 4
VT_PAD = 16


def _attn_kernel(seq, q_ref, k_ref, v_ref, sink_ref, o_ref, ks_ref, vt_ref):
    n = pl.program_id(1)
    span = 3 * QBLOCK
    heads_per_tile = LANES // HEAD_DIM
    tiles_per_group = GROUP // heads_per_tile

    @pl.when(n == 0)
    def _():
        def prep(j, carry):
            r = pl.ds(pl.multiple_of(j * QBLOCK, QBLOCK), QBLOCK)
            upper = lax.broadcasted_iota(jnp.int32, (QBLOCK, LANES), 1) >= HEAD_DIM
            ones_tile = jnp.where(lax.broadcasted_iota(jnp.int32, (VT_PAD, QBLOCK), 0) == 0, 1.0, 0.0)
            for t in range(N_KV_HEADS // heads_per_tile):
                kt = k_ref[r, pl.ds(t * LANES, LANES)].astype(F32)
                vt = v_ref[r, pl.ds(t * LANES, LANES)].astype(F32).T
                for hh in range(heads_per_tile):
                    g = t * heads_per_tile + hh
                    k_own = jnp.where(upper if hh else jnp.logical_not(upper), kt, 0.0)
                    ks_ref[g, r, :] = (k_own + pltpu.roll(k_own, HEAD_DIM, 1)).astype(BF16)
                    vt_ref[g, j, pl.ds(0, HEAD_DIM), :] = vt[hh * HEAD_DIM:(hh + 1) * HEAD_DIM].astype(BF16)
                    vt_ref[g, j, pl.ds(HEAD_DIM, VT_PAD), :] = ones_tile.astype(BF16)
            return carry

        lax.fori_loop(0, seq // QBLOCK, prep, 0, unroll=PREP_UNROLL)

    q0 = pl.multiple_of(n * QBLOCK, QBLOCK)
    start = pl.multiple_of(jnp.clip(q0 - QBLOCK, 0, seq - span), QBLOCK)
    first_block = start // QBLOCK
    kpos = start + lax.broadcasted_iota(jnp.int32, (span, LANES), 0)
    qpos = q0 + lax.broadcasted_iota(jnp.int32, (span, LANES), 1)
    bias = jnp.where(jnp.abs(qpos - kpos) <= WINDOW, 0.0, -jnp.inf)
    cols = GROUP * QBLOCK
    bias = jnp.concatenate([bias] * GROUP, axis=1)
    col = lax.broadcasted_iota(jnp.int32, (1, cols), 1)
    upper = lax.broadcasted_iota(jnp.int32, (tiles_per_group * QBLOCK, LANES), 1) >= HEAD_DIM
    scores, weights = [], []
    for g in range(N_KV_HEADS):
        qs = jnp.concatenate([q_ref[:, pl.ds((g * tiles_per_group + t) * LANES, LANES)]
                              for t in range(tiles_per_group)], axis=0)
        zero = jnp.zeros_like(qs)
        rhs = jnp.concatenate([jnp.where(upper, zero, qs), jnp.where(upper, qs, zero)], axis=0)
        scores.append(_dot_nt(ks_ref[g, pl.ds(start, span), :], rhs))
    for g in range(N_KV_HEADS):
        sc = scores[g] + bias
        sink = sink_ref[g * GROUP]
        for p in range(heads_per_tile):
            for t in range(tiles_per_group):
                if p or t:
                    sink = jnp.where(col >= (p * tiles_per_group + t) * QBLOCK,
                                     sink_ref[g * GROUP + t * heads_per_tile + p], sink)
        m = jnp.maximum(jnp.max(sc, axis=0, keepdims=True), sink)
        weights.append((jnp.exp((sc - m).astype(BF16)), jnp.exp(sink - m)))
    for g in range(N_KV_HEADS):
        e, e_sink = weights[g]
        pv = jnp.zeros((HEAD_DIM + VT_PAD, cols), F32)
        for j in range(span // QBLOCK):
            pv = pv + _dot(vt_ref[g, first_block + j], e[j * QBLOCK:(j + 1) * QBLOCK, :])
        out = pv[:HEAD_DIM] * (1.0 / (pv[HEAD_DIM:HEAD_DIM + 1] + e_sink))
        for t in range(tiles_per_group):
            tile = jnp.concatenate(
                [out[:, (p * tiles_per_group + t) * QBLOCK:(p * tiles_per_group + t + 1) * QBLOCK]
                 for p in range(heads_per_tile)], axis=0)
            o_ref[:, pl.ds((g * tiles_per_group + t) * LANES, LANES)] = tile.T.astype(o_ref.dtype)


def _attention(qkv, batch, seq, sinks):
    nb = seq // QBLOCK
    q_w = N_Q_HEADS * HEAD_DIM
    kv_w = N_KV_HEADS * HEAD_DIM
    return pl.pallas_call(
        functools.partial(_attn_kernel, seq),
        grid=(batch, nb),
        in_specs=[
            pl.BlockSpec((QBLOCK, q_w), lambda b, n: (b * nb + n, 0)),
            pl.BlockSpec((seq, kv_w), lambda b, n: (b, q_w // kv_w)),
            pl.BlockSpec((seq, kv_w), lambda b, n: (b, q_w // kv_w + 1)),
            pl.BlockSpec(memory_space=pltpu.SMEM),
        ],
        out_specs=pl.BlockSpec((QBLOCK, q_w), lambda b, n: (b * nb + n, 0)),
        out_shape=jax.ShapeDtypeStruct((batch * seq, q_w), BF16),
        scratch_shapes=[pltpu.VMEM((N_KV_HEADS, seq, LANES), BF16),
                        pltpu.VMEM((N_KV_HEADS, nb, HEAD_DIM + VT_PAD, QBLOCK), BF16)],
        compiler_params=pltpu.CompilerParams(
            dimension_semantics=("parallel", "arbitrary"), vmem_limit_bytes=VMEM_LIMIT),
        name="window_attention",
    )(qkv, qkv, qkv, sinks)


def _rope_tables(seq):
    pos = jnp.arange(seq, dtype=F32)
    inv_freq = ROPE_THETA ** (-jnp.arange(0, HEAD_DIM, 2, dtype=F32) / HEAD_DIM)
    ang = pos[:, None] * inv_freq[None, :]
    cos, sin = jnp.cos(ang), jnp.sin(ang)
    reps = LANES // HEAD_DIM
    return (jnp.tile(jnp.concatenate([cos, cos], axis=1), (1, reps)),
            jnp.tile(jnp.concatenate([-sin, sin], axis=1), (1, reps)))


def kernel(x, norm_g, rec_w_in, rec_conv_w, rec_conv_b, rg_w_r, rg_b_r, rg_w_i, rg_b_i, rg_lambda,
           hgrn_lb_logits, hgrn_norm_g, rec_w_out, att_w_qkv, att_sinks, att_w_o, mlp_w1, mlp_w2):
    batch, seq, d = x.shape
    cos_t, sin_t = _rope_tables(seq)
    h = x.reshape(batch * seq, d)
    w_att_out = att_w_o.astype(BF16)
    w_in_all, w_qkv_all = rec_w_in, att_w_qkv
    w1_all, w2_all = mlp_w1, mlp_w2
    w_rec_out = (rec_w_out.reshape(N_REC_LAYERS, 2, A_HEADS, A_BLOCK, d).transpose(0, 2, 1, 3, 4)
                 .reshape(N_REC_LAYERS, D_A + D_B, d).astype(BF16))
    for layer in range(DEPTH):
        g = norm_g[layer]
        if layer % 2 == 0:
            r = layer // 2
            proj = _norm_matmul(h, g[0], w_in_all, r, BF16)
            w_gate = jnp.concatenate([rg_w_r[r, 0], rg_w_i[r, 0], rg_w_r[r, 1], rg_w_i[r, 1]], axis=-1).astype(BF16)
            b_gate = jnp.stack([rg_b_r[r, 0], rg_b_i[r, 0], rg_b_r[r, 1], rg_b_i[r, 1]], axis=0)
            b_gate = b_gate.reshape(4, A_HEADS, A_BLOCK).transpose(1, 0, 2).reshape(A_HEADS, 1, 4 * A_BLOCK)
            y = _rec_mixer(r, proj, batch, seq, rec_conv_w[r], rec_conv_b[r], w_gate, b_gate, rg_lambda[r],
                           hgrn_lb_logits, hgrn_norm_g[r])
            w_out, o_idx = w_rec_out, r
        else:
            a = layer // 2
            qkv = _qkv_proj(h, g[0], w_qkv_all, a, cos_t, sin_t, seq)
            y = _attention(qkv, batch, seq, att_sinks[a])
            w_out, o_idx = w_att_out, a
        h = _out_mlp(y, w_out, o_idx, g[1], h, g[2], w1_all, w2_all, layer, g[3])
    return h.reshape(batch, seq, d)
```

```python
import functools

import jax
import jax.numpy as jnp
from jax import lax
from jax.experimental import pallas as pl
from jax.experimental.pallas import tpu as pltpu

D_MODEL = 1024
DEPTH = 4
N_REC_LAYERS = (DEPTH + 1) // 2
D_A = D_MODEL // 2
A_HEADS = 4
A_BLOCK = D_A // A_HEADS
CONV_WIDTH = 4
RGLRU_C = 8.0
D_B = D_MODEL // 2
B_HEADS = 4
B_DK = D_B // B_HEADS
HGRN_CHUNK = 32
REC_IN = 2 * D_A + 5 * D_B
HEAD_DIM = 64
N_Q_HEADS = D_MODEL // HEAD_DIM
N_KV_HEADS = 4
GROUP = N_Q_HEADS // N_KV_HEADS
WINDOW = 128
QBLOCK = 128
ROPE_THETA = 10000.0
QKV_OUT = (N_Q_HEADS + 2 * N_KV_HEADS) * HEAD_DIM
D_FF = 4 * D_MODEL
EPS = 1e-6

LANES = 128
SUBLANES = 8
ROW_TILE = 1024
MLP_ROW_TILE = 512
DOT_COLS = 1024
MIX_ROWS = 256
MIX_HEADS = 2
SEG_PITCH = MIX_ROWS + SUBLANES
SCAN_UNROLL = 8
VMEM_LIMIT = 52 * 1024 * 1024

BF16 = jnp.bfloat16
F32 = jnp.float32


def _rms(x, g):
    return x * lax.rsqrt(jnp.mean(x * x, axis=-1, keepdims=True) + EPS) * g


def _dot(a, b):
    return jnp.dot(a, b, preferred_element_type=F32)


def _dot_nt(a, b):
    return lax.dot_general(a, b, (((1,), (1,)), ((), ())), preferred_element_type=F32)


def _dot_tn(a, b):
    return lax.dot_general(a, b, (((0,), (0,)), ((), ())), preferred_element_type=F32)


def _rec_proj_kernel(x_ref, g_ref, w_ref, o_ref):
    xn = _rms(x_ref[...], g_ref[...]).astype(BF16)
    n = o_ref.shape[1]
    for c0 in range(0, n, DOT_COLS):
        c1 = min(c0 + DOT_COLS, n)
        res = _dot(xn, w_ref[:, c0:c1].astype(BF16))
        for t0 in range(c0, c1, LANES):
            tile = res[:, t0 - c0:t0 - c0 + LANES]
            if D_A <= t0 < 2 * D_A:
                tile = _gelu_tanh(tile)
            elif t0 >= REC_IN - D_B:
                tile = tile * _sigmoid(tile)
            o_ref[:, t0:t0 + LANES] = tile.astype(o_ref.dtype)


def _rope(t, cos, sin):
    w = t.shape[1]
    lane = lax.broadcasted_iota(jnp.int32, t.shape, 1)
    rot = jnp.where(lane % HEAD_DIM < HEAD_DIM // 2,
                    pltpu.roll(t, w - HEAD_DIM // 2, 1), pltpu.roll(t, HEAD_DIM // 2, 1))
    return t * cos + rot * sin


def _qkv_proj_kernel(x_ref, g_ref, w_ref, cos_ref, sin_ref, o_ref):
    xn = _rms(x_ref[...], g_ref[...]).astype(BF16)
    q_w = N_Q_HEADS * HEAD_DIM
    rope_w = q_w + N_KV_HEADS * HEAD_DIM
    cos, sin = cos_ref[...], sin_ref[...]
    n = o_ref.shape[1]
    for c0 in range(0, n, DOT_COLS):
        c1 = min(c0 + DOT_COLS, n)
        res = _dot(xn, w_ref[:, c0:c1].astype(BF16))
        for t0 in range(c0, c1, LANES):
            tile = res[:, t0 - c0:t0 - c0 + LANES]
            if t0 < rope_w:
                tile = _rope(tile, cos, sin)
            if t0 < q_w:
                tile = tile * HEAD_DIM ** -0.5
            o_ref[:, t0:t0 + LANES] = tile.astype(o_ref.dtype)


def _qkv_proj(x, g, w_stack, idx, cos_t, sin_t, seq):
    m, k = x.shape
    n = w_stack.shape[2]
    tiles_per_seq = seq // ROW_TILE
    table = pl.BlockSpec((ROW_TILE, LANES), lambda i: (i % tiles_per_seq, 0))
    return pl.pallas_call(
        _qkv_proj_kernel,
        grid=(m // ROW_TILE,),
        in_specs=[
            pl.BlockSpec((ROW_TILE, k), lambda i: (i, 0)),
            pl.BlockSpec((1, k), lambda i: (0, 0)),
            pl.BlockSpec((None, k, n), lambda i: (idx, 0, 0), pipeline_mode=pl.Buffered(1)),
            table, table,
        ],
        out_specs=pl.BlockSpec((ROW_TILE, n), lambda i: (i, 0)),
        out_shape=jax.ShapeDtypeStruct((m, n), BF16),
        compiler_params=pltpu.CompilerParams(
            dimension_semantics=("parallel",), vmem_limit_bytes=VMEM_LIMIT),
        name="qkv_proj",
    )(x, g.reshape(1, k), w_stack, cos_t, sin_t)


def _rec_proj(x, g, w_stack, idx, out_dtype):
    m, k = x.shape
    n = w_stack.shape[2]
    w = w_stack
    return pl.pallas_call(
        _rec_proj_kernel,
        grid=(m // ROW_TILE,),
        in_specs=[
            pl.BlockSpec((ROW_TILE, k), lambda i: (i, 0)),
            pl.BlockSpec((1, k), lambda i: (0, 0)),
            pl.BlockSpec((None, k, n), lambda i: (idx, 0, 0), pipeline_mode=pl.Buffered(1)),
        ],
        out_specs=pl.BlockSpec((ROW_TILE, n), lambda i: (i, 0)),
        out_shape=jax.ShapeDtypeStruct((m, n), out_dtype),
        compiler_params=pltpu.CompilerParams(
            dimension_semantics=("parallel",), vmem_limit_bytes=VMEM_LIMIT),
        name="rec_proj",
    )(x, g.reshape(1, k), w)


def _out_mlp_kernel(a_ref, wo_ref, g_mix_ref, h_ref, g_in_ref, w1_ref, w2_ref, g_out_ref, o_ref):
    h1 = h_ref[...] + _rms(_dot(a_ref[...], wo_ref[...]), g_mix_ref[...])
    xn = _rms(h1, g_in_ref[...]).astype(BF16)
    ff = w1_ref.shape[1]
    acc = None
    for c0 in range(0, ff, DOT_COLS):
        hid = jnp.maximum(_dot(xn, w1_ref[:, c0:c0 + DOT_COLS].astype(BF16)), 0.0)
        part = _dot((hid * hid).astype(BF16), w2_ref[c0:c0 + DOT_COLS, :].astype(BF16))
        acc = part if acc is None else acc + part
    o_ref[...] = h1 + _rms(acc, g_out_ref[...])


def _out_mlp(a, w_o, o_idx, g_mix, h, g_in, w1, w2, layer, g_out):
    m, d = h.shape
    k = a.shape[1]
    ff = w1.shape[2]
    vec = pl.BlockSpec((1, d), lambda i: (0, 0))
    resident = lambda shape, idx: pl.BlockSpec((None,) + shape, lambda i: (idx, 0, 0),
                                               pipeline_mode=pl.Buffered(1))
    return pl.pallas_call(
        _out_mlp_kernel,
        grid=(m // MLP_ROW_TILE,),
        in_specs=[
            pl.BlockSpec((MLP_ROW_TILE, k), lambda i: (i, 0)),
            resident((k, d), o_idx),
            vec,
            pl.BlockSpec((MLP_ROW_TILE, d), lambda i: (i, 0)),
            vec,
            resident((d, ff), layer),
            resident((ff, d), layer),
            vec,
        ],
        out_specs=pl.BlockSpec((MLP_ROW_TILE, d), lambda i: (i, 0)),
        out_shape=jax.ShapeDtypeStruct((m, d), F32),
        compiler_params=pltpu.CompilerParams(
            dimension_semantics=("parallel",), vmem_limit_bytes=VMEM_LIMIT),
        name="out_mlp",
    )(a, w_o, g_mix.reshape(1, d), h, g_in.reshape(1, d), w1, w2, g_out.reshape(1, d))


def _shift_down(x, k):
    return pltpu.roll(x, k, 0)


def _shift_up(x, k):
    return pltpu.roll(x, x.shape[0] - k, 0)


def _chunk_sum_matrix():
    i = jnp.arange(MIX_ROWS)[:, None]
    j = jnp.arange(MIX_ROWS)[None, :]
    same = (i // HGRN_CHUNK) == (j // HGRN_CHUNK)
    return jnp.concatenate([same & (j <= i), same & (j > i)], axis=0).astype(BF16)


def _chunk_causal_masks():
    i = jnp.arange(MIX_ROWS)[:, None]
    j = jnp.arange(MIX_ROWS)[None, :]
    same = (i // HGRN_CHUNK) == (j // HGRN_CHUNK)
    return jnp.stack([same & (j <= i), same & (j >= i)]).astype(F32)


def _sublane_linear_scan(a, u, reverse):
    pos = lax.broadcasted_iota(jnp.int32, a.shape, 0)
    k = 1
    while k < SUBLANES:
        if reverse:
            keep = pos < SUBLANES - k
            a_n, u_n = _shift_up(a, k), _shift_up(u, k)
        else:
            keep = pos >= k
            a_n, u_n = _shift_down(a, k), _shift_down(u, k)
        u = u + jnp.where(keep, a * u_n, 0.0)
        a = a * jnp.where(keep, a_n, 1.0)
        k *= 2
    return u


def _sigmoid(x):
    return 0.5 * jnp.tanh(0.5 * x) + 0.5


def _softplus(x):
    return jnp.maximum(x, 0.0) + jnp.log1p(jnp.exp(-jnp.abs(x)))


def _gelu_tanh(x):
    return 0.5 * x * (1.0 + jnp.tanh(0.7978845608028654 * (x + 0.044715 * (x * x * x))))


def _rec_mixer_kernel(layer, seq,
                      xa_ref, ga_ref, q_ref, zf_ref, zb_ref, iv_ref, g_ref,
                      cw_ref, cb_ref, wg_ref, bg_ref, lam_ref, lbl_ref, ng_ref, tri_ref, mask_ref,
                      y_ref,
                      xpad_ref, a_ref, u_ref, pcum_ref, hloc_ref, cin_ref, o_ref, st_ref):
    n_blocks = seq // MIX_ROWS
    chunks = MIX_ROWS // HGRN_CHUNK
    assert n_blocks == SUBLANES
    zeros8 = jnp.zeros((SUBLANES, LANES), F32)
    ones8 = jnp.ones((SUBLANES, LANES), F32)
    heads = range(MIX_HEADS)
    lane = [pl.ds(hd * LANES, LANES) for hd in heads]

    lbs, sps = [], []
    for hd in heads:
        logits = lbl_ref[:, :, lane[hd]]
        e = jnp.exp(logits - jnp.max(logits, axis=1, keepdims=True))
        s_lb = e / jnp.sum(e, axis=1, keepdims=True)
        lb = jnp.zeros((2, LANES), F32)
        for r in range(1, layer + 1):
            lb = lb + s_lb[:, r, :]
        lbs.append(lb)
        sps.append(_softplus(-lam_ref[:, lane[hd]]))
        xpad_ref[hd, pl.ds(0, SUBLANES), :] = zeros8
        xpad_ref[hd, pl.ds(seq + SUBLANES, SUBLANES), :] = zeros8
        xpad_ref[hd, pl.ds(SUBLANES, seq), :] = xa_ref[:, lane[hd]].astype(F32)
    st_ref[...] = jnp.zeros_like(st_ref)

    def rglru_inputs(hd, c):
        r0 = pl.multiple_of(c * MIX_ROWS, MIX_ROWS)
        seg_rows = pl.ds(pl.multiple_of(c * SEG_PITCH, SUBLANES), MIX_ROWS)
        win = xpad_ref[hd, pl.ds(r0, MIX_ROWS + 2 * SUBLANES), :]
        wrows = MIX_ROWS + 2 * SUBLANES
        xc = jnp.zeros((MIX_ROWS, LANES), F32) + cb_ref[:, lane[hd]]
        for k in range(CONV_WIDTH):
            shift = (2 - k) % wrows
            shifted = (pltpu.roll(win, shift, 0) if shift else win)[SUBLANES:SUBLANES + MIX_ROWS]
            xc = xc + cw_ref[pl.ds(k, 1), lane[hd]] * shifted
        gates = _dot(xc.astype(BF16), wg_ref[hd]) + bg_ref[hd]
        for d in range(2):
            r = _sigmoid(gates[:, (2 * d) * LANES:(2 * d + 1) * LANES])
            i = _sigmoid(gates[:, (2 * d + 1) * LANES:(2 * d + 2) * LANES])
            log_a = (-RGLRU_C) * r * sps[hd][d:d + 1, :]
            a = jnp.exp(log_a)
            a_ref[hd, d, seg_rows, :] = a
            u_ref[hd, d, seg_rows, :] = jnp.sqrt(-jnp.tanh(log_a) * (1.0 + a * a)) * (i * xc)

    def vector_phase(c, carry):
        for hd in heads:
            rglru_inputs(hd, c)
            hgrn_block(hd, c)
        return carry

    def hgrn_block(hd, c):
        lb = lbs[hd]
        rows = [pl.ds(pl.multiple_of(blk * MIX_ROWS, MIX_ROWS), MIX_ROWS) for blk in (c, n_blocks - 1 - c)]
        log_fs, kks, parts = [], [], []
        for d, z_ref in enumerate((zf_ref, zb_ref)):
            sig = _sigmoid(z_ref[rows[d], lane[hd]].astype(F32))
            lbd = lb[d:d + 1, :]
            log_f = jnp.log(lbd + (1.0 - lbd) * sig)
            log_fs.append(log_f)
            kks.append((1.0 - lbd) * (1.0 - sig))
            hi = log_f.astype(BF16)
            parts += [hi, (log_f - hi.astype(F32)).astype(BF16)]
        sums = _dot(tri_ref[...], jnp.concatenate(parts, axis=1))
        pre_f = sums[:MIX_ROWS, 0:LANES] + sums[:MIX_ROWS, LANES:2 * LANES]
        suf_f = sums[MIX_ROWS:, 0:LANES] + sums[MIX_ROWS:, LANES:2 * LANES]
        pre_b = sums[:MIX_ROWS, 2 * LANES:3 * LANES] + sums[:MIX_ROWS, 3 * LANES:]
        suf_b = sums[MIX_ROWS:, 2 * LANES:3 * LANES] + sums[MIX_ROWS:, 3 * LANES:]
        bcums = (pre_f, suf_b + log_fs[1])
        rests = (suf_f, pre_b - log_fs[1])
        chunk_slices = [slice(j * HGRN_CHUNK, (j + 1) * HGRN_CHUNK) for j in range(chunks)]
        qes, decs, o_ins, upds = [], [], [], []
        for d in range(2):
            bcum, rest, kk = bcums[d], rests[d], kks[d]
            vv = iv_ref[rows[d], lane[hd]].astype(BF16)
            qe = (q_ref[rows[d], lane[hd]].astype(F32) * jnp.exp(bcum)).astype(BF16)
            ke = (kk * jnp.exp(-bcum)).astype(BF16)
            kd = (kk * jnp.exp(rest)).astype(BF16)
            att = jnp.where(mask_ref[d] > 0.5, _dot_nt(qe, ke), 0.0)
            qes.append(qe)
            o_ins.append(_dot(att.astype(BF16), vv))
            decs.append([jnp.exp(bcum[ch.start:ch.start + 1, :] + rest[ch.start:ch.start + 1, :])
                         for ch in chunk_slices])
            upds.append([_dot_tn(vv[ch], kd[ch]) for ch in chunk_slices])
        entering = []
        for d in range(2):
            st = st_ref[hd, d]
            states = [None] * chunks
            for j in (range(chunks) if d == 0 else reversed(range(chunks))):
                states[j] = st.astype(BF16)
                st = decs[d][j] * st + upds[d][j]
            st_ref[hd, d] = st
            entering.append(states)
        for d in range(2):
            pieces = [_dot_nt(qes[d][ch], entering[d][j]) for j, ch in enumerate(chunk_slices)]
            o_ref[hd, d, rows[d], :] = o_ins[d] + jnp.concatenate(pieces, axis=0)

    lax.fori_loop(0, n_blocks, vector_phase, 0)

    def scan_phase(t, carry):
        out = []
        for hd in heads:
            hf, pf, hb, pb = carry[4 * hd:4 * hd + 4]
            tb = MIX_ROWS - 1 - t
            af = a_ref[hd, 0, pl.ds(t, SUBLANES, stride=SEG_PITCH), :]
            hf = af * hf + u_ref[hd, 0, pl.ds(t, SUBLANES, stride=SEG_PITCH), :]
            pf = pf * af
            hloc_ref[hd, 0, pl.ds(t, SUBLANES, stride=SEG_PITCH), :] = hf
            pcum_ref[hd, 0, pl.ds(t, SUBLANES, stride=SEG_PITCH), :] = pf
            ab = a_ref[hd, 1, pl.ds(tb, SUBLANES, stride=SEG_PITCH), :]
            hb = ab * hb + u_ref[hd, 1, pl.ds(tb, SUBLANES, stride=SEG_PITCH), :]
            pb = pb * ab
            hloc_ref[hd, 1, pl.ds(tb, SUBLANES, stride=SEG_PITCH), :] = hb
            pcum_ref[hd, 1, pl.ds(tb, SUBLANES, stride=SEG_PITCH), :] = pb
            out += [hf, pf, hb, pb]
        return tuple(out)

    totals = lax.fori_loop(0, MIX_ROWS, scan_phase, (zeros8, ones8, zeros8, ones8) * MIX_HEADS,
                           unroll=SCAN_UNROLL)
    sub = lax.broadcasted_iota(jnp.int32, (SUBLANES, LANES), 0)
    for hd in heads:
        hf, pf, hb, pb = totals[4 * hd:4 * hd + 4]
        cin_ref[hd, 0] = jnp.where(sub >= 1, _shift_down(_sublane_linear_scan(pf, hf, False), 1), 0.0)
        cin_ref[hd, 1] = jnp.where(sub < SUBLANES - 1, _shift_up(_sublane_linear_scan(pb, hb, True), 1), 0.0)

    def out_phase(c, carry):
        rows = pl.ds(pl.multiple_of(c * MIX_ROWS, MIX_ROWS), MIX_ROWS)
        seg_rows = pl.ds(pl.multiple_of(c * SEG_PITCH, SUBLANES), MIX_ROWS)
        for hd in heads:
            h = (hloc_ref[hd, 0, seg_rows, :] + pcum_ref[hd, 0, seg_rows, :] * cin_ref[hd, 0, pl.ds(c, 1), :]
                 + hloc_ref[hd, 1, seg_rows, :] + pcum_ref[hd, 1, seg_rows, :] * cin_ref[hd, 1, pl.ds(c, 1), :])
            y_a = ga_ref[rows, lane[hd]].astype(F32) * h
            y_b = (_rms(o_ref[hd, 0, rows, :] + o_ref[hd, 1, rows, :], ng_ref[:, lane[hd]])
                   * g_ref[rows, lane[hd]].astype(F32))
            y_ref[rows, pl.ds(2 * hd * LANES, LANES)] = y_a.astype(y_ref.dtype)
            y_ref[rows, pl.ds((2 * hd + 1) * LANES, LANES)] = y_b.astype(y_ref.dtype)
        return carry

    lax.fori_loop(0, n_blocks, out_phase, 0)


def _rec_mixer(layer, proj, batch, seq, conv_w, conv_b, w_gate, b_gate, lam, lb_logits, hg_norm):
    nh = A_HEADS
    hs = MIX_HEADS
    w = hs * LANES
    groups = nh // hs
    col = lambda part: pl.BlockSpec((seq, w), lambda b, h, part=part: (b, part * groups + h))
    in_specs = [col(0), col(1), col(2), col(3), col(4), col(5), col(6),
                pl.BlockSpec((CONV_WIDTH, w), lambda b, h: (0, h)),
                pl.BlockSpec((1, w), lambda b, h: (0, h)),
                pl.BlockSpec((hs, LANES, 4 * LANES), lambda b, h: (h, 0, 0)),
                pl.BlockSpec((hs, 1, 4 * LANES), lambda b, h: (h, 0, 0)),
                pl.BlockSpec((2, w), lambda b, h: (0, h)),
                pl.BlockSpec((2, N_REC_LAYERS, w), lambda b, h: (0, 0, h)),
                pl.BlockSpec((1, w), lambda b, h: (0, h)),
                pl.BlockSpec((2 * MIX_ROWS, MIX_ROWS), lambda b, h: (0, 0)),
                pl.BlockSpec((2, MIX_ROWS, MIX_ROWS), lambda b, h: (0, 0, 0))]
    seg_total = (seq // MIX_ROWS) * SEG_PITCH
    return pl.pallas_call(
        functools.partial(_rec_mixer_kernel, layer, seq),
        grid=(batch, groups),
        in_specs=in_specs,
        out_specs=pl.BlockSpec((seq, 2 * w), lambda b, h: (b, h)),
        out_shape=jax.ShapeDtypeStruct((batch * seq, D_A + D_B), BF16),
        scratch_shapes=[
            pltpu.VMEM((hs, seq + 2 * SUBLANES, LANES), F32),
            pltpu.VMEM((hs, 2, seg_total, LANES), F32),
            pltpu.VMEM((hs, 2, seg_total, LANES), F32),
            pltpu.VMEM((hs, 2, seg_total, LANES), F32),
            pltpu.VMEM((hs, 2, seg_total, LANES), F32),
            pltpu.VMEM((hs, 2, SUBLANES, LANES), F32),
            pltpu.VMEM((hs, 2, seq, LANES), F32),
            pltpu.VMEM((hs, 2, LANES, LANES), F32),
        ],
        compiler_params=pltpu.CompilerParams(
            dimension_semantics=("parallel", "parallel"), vmem_limit_bytes=VMEM_LIMIT),
        name="rec_mixer",
    )(proj, proj, proj, proj, proj, proj, proj,
      conv_w, conv_b.reshape(1, D_A), w_gate, b_gate, lam, lb_logits, hg_norm.reshape(1, D_B),
      _chunk_sum_matrix(), _chunk_causal_masks())


ATTN_QBLOCKS = 4
PREP_UNROLL = 4
VT_PAD = 16


def _attn_kernel(seq, q_ref, k_ref, v_ref, sink_ref, o_ref, ks_ref, vt_ref):
    n = pl.program_id(1)
    span = 3 * QBLOCK
    heads_per_tile = LANES // HEAD_DIM
    tiles_per_group = GROUP // heads_per_tile

    @pl.when(n == 0)
    def _():
        def prep(j, carry):
            r = pl.ds(pl.multiple_of(j * QBLOCK, QBLOCK), QBLOCK)
            upper = lax.broadcasted_iota(jnp.int32, (QBLOCK, LANES), 1) >= HEAD_DIM
            ones_tile = jnp.where(lax.broadcasted_iota(jnp.int32, (VT_PAD, QBLOCK), 0) == 0, 1.0, 0.0)
            for t in range(N_KV_HEADS // heads_per_tile):
                kt = k_ref[r, pl.ds(t * LANES, LANES)].astype(F32)
                vt = v_ref[r, pl.ds(t * LANES, LANES)].astype(F32).T
                for hh in range(heads_per_tile):
                    g = t * heads_per_tile + hh
                    k_own = jnp.where(upper if hh else jnp.logical_not(upper), kt, 0.0)
                    ks_ref[g, r, :] = (k_own + pltpu.roll(k_own, HEAD_DIM, 1)).astype(BF16)
                    vt_ref[g, j, pl.ds(0, HEAD_DIM), :] = vt[hh * HEAD_DIM:(hh + 1) * HEAD_DIM].astype(BF16)
                    vt_ref[g, j, pl.ds(HEAD_DIM, VT_PAD), :] = ones_tile.astype(BF16)
            return carry

        lax.fori_loop(0, seq // QBLOCK, prep, 0, unroll=PREP_UNROLL)

    cols = GROUP * QBLOCK
    col = lax.broadcasted_iota(jnp.int32, (1, cols), 1)
    upper = lax.broadcasted_iota(jnp.int32, (tiles_per_group * QBLOCK, LANES), 1) >= HEAD_DIM
    subs = range(ATTN_QBLOCKS)
    starts, biases = [], []
    for s in subs:
        q0 = pl.multiple_of((n * ATTN_QBLOCKS + s) * QBLOCK, QBLOCK)
        start = pl.multiple_of(jnp.clip(q0 - QBLOCK, 0, seq - span), QBLOCK)
        kpos = start + lax.broadcasted_iota(jnp.int32, (span, LANES), 0)
        qpos = q0 + lax.broadcasted_iota(jnp.int32, (span, LANES), 1)
        bias = jnp.where(jnp.abs(qpos - kpos) <= WINDOW, 0.0, -jnp.inf)
        starts.append(start)
        biases.append(jnp.concatenate([bias] * GROUP, axis=1))
    scores, weights = {}, {}
    for s in subs:
        q_rows = pl.ds(s * QBLOCK, QBLOCK)
        for g in range(N_KV_HEADS):
            qs = jnp.concatenate([q_ref[q_rows, pl.ds((g * tiles_per_group + t) * LANES, LANES)]
                                  for t in range(tiles_per_group)], axis=0)
            zero = jnp.zeros_like(qs)
            rhs = jnp.concatenate([jnp.where(upper, zero, qs), jnp.where(upper, qs, zero)], axis=0)
            scores[s, g] = _dot_nt(ks_ref[g, pl.ds(starts[s], span), :], rhs)
    for s in subs:
        for g in range(N_KV_HEADS):
            sc = scores[s, g] + biases[s]
            sink = sink_ref[g * GROUP]
            for p in range(heads_per_tile):
                for t in range(tiles_per_group):
                    if p or t:
                        sink = jnp.where(col >= (p * tiles_per_group + t) * QBLOCK,
                                         sink_ref[g * GROUP + t * heads_per_tile + p], sink)
            m = jnp.maximum(jnp.max(sc, axis=0, keepdims=True), sink)
            weights[s, g] = (jnp.exp((sc - m).astype(BF16)), jnp.exp(sink - m))
    for s in subs:
        first_block = starts[s] // QBLOCK
        for g in range(N_KV_HEADS):
            e, e_sink = weights[s, g]
            pv = jnp.zeros((HEAD_DIM + VT_PAD, cols), F32)
            for j in range(span // QBLOCK):
                pv = pv + _dot(vt_ref[g, first_block + j], e[j * QBLOCK:(j + 1) * QBLOCK, :])
            out = pv[:HEAD_DIM] * (1.0 / (pv[HEAD_DIM:HEAD_DIM + 1] + e_sink))
            for t in range(tiles_per_group):
                tile = jnp.concatenate(
                    [out[:, (p * tiles_per_group + t) * QBLOCK:(p * tiles_per_group + t + 1) * QBLOCK]
                     for p in range(heads_per_tile)], axis=0)
                o_ref[pl.ds(s * QBLOCK, QBLOCK), pl.ds((g * tiles_per_group + t) * LANES, LANES)] = (
                    tile.T.astype(o_ref.dtype))


def _attention(qkv, batch, seq, sinks):
    nb = seq // QBLOCK
    steps = nb // ATTN_QBLOCKS
    q_w = N_Q_HEADS * HEAD_DIM
    kv_w = N_KV_HEADS * HEAD_DIM
    return pl.pallas_call(
        functools.partial(_attn_kernel, seq),
        grid=(batch, steps),
        in_specs=[
            pl.BlockSpec((ATTN_QBLOCKS * QBLOCK, q_w), lambda b, n: (b * steps + n, 0)),
            pl.BlockSpec((seq, kv_w), lambda b, n: (b, q_w // kv_w)),
            pl.BlockSpec((seq, kv_w), lambda b, n: (b, q_w // kv_w + 1)),
            pl.BlockSpec(memory_space=pltpu.SMEM),
        ],
        out_specs=pl.BlockSpec((ATTN_QBLOCKS * QBLOCK, q_w), lambda b, n: (b * steps + n, 0)),
        out_shape=jax.ShapeDtypeStruct((batch * seq, q_w), BF16),
        scratch_shapes=[pltpu.VMEM((N_KV_HEADS, seq, LANES), BF16),
                        pltpu.VMEM((N_KV_HEADS, nb, HEAD_DIM + VT_PAD, QBLOCK), BF16)],
        compiler_params=pltpu.CompilerParams(
            dimension_semantics=("parallel", "arbitrary"), vmem_limit_bytes=VMEM_LIMIT),
        name="window_attention",
    )(qkv, qkv, qkv, sinks)


def _rope_tables(seq):
    pos = jnp.arange(seq, dtype=F32)
    inv_freq = ROPE_THETA ** (-jnp.arange(0, HEAD_DIM, 2, dtype=F32) / HEAD_DIM)
    ang = pos[:, None] * inv_freq[None, :]
    cos, sin = jnp.cos(ang), jnp.sin(ang)
    reps = LANES // HEAD_DIM
    return (jnp.tile(jnp.concatenate([cos, cos], axis=1), (1, reps)),
            jnp.tile(jnp.concatenate([-sin, sin], axis=1), (1, reps)))


def kernel(x, norm_g, rec_w_in, rec_conv_w, rec_conv_b, rg_w_r, rg_b_r, rg_w_i, rg_b_i, rg_lambda,
           hgrn_lb_logits, hgrn_norm_g, rec_w_out, att_w_qkv, att_sinks, att_w_o, mlp_w1, mlp_w2):
    batch, seq, d = x.shape
    cos_t, sin_t = _rope_tables(seq)
    h = x.reshape(batch * seq, d)
    w_att_out = att_w_o.astype(BF16)
    w_in_all, w_qkv_all = rec_w_in, att_w_qkv
    w1_all, w2_all = mlp_w1, mlp_w2
    w_rec_out = (rec_w_out.reshape(N_REC_LAYERS, 2, A_HEADS, A_BLOCK, d).transpose(0, 2, 1, 3, 4)
                 .reshape(N_REC_LAYERS, D_A + D_B, d).astype(BF16))
    for layer in range(DEPTH):
        g = norm_g[layer]
        if layer % 2 == 0:
            r = layer // 2
            proj = _rec_proj(h, g[0], w_in_all, r, BF16)
            w_gate = jnp.concatenate([rg_w_r[r, 0], rg_w_i[r, 0], rg_w_r[r, 1], rg_w_i[r, 1]], axis=-1).astype(BF16)
            b_gate = jnp.stack([rg_b_r[r, 0], rg_b_i[r, 0], rg_b_r[r, 1], rg_b_i[r, 1]], axis=0)
            b_gate = b_gate.reshape(4, A_HEADS, A_BLOCK).transpose(1, 0, 2).reshape(A_HEADS, 1, 4 * A_BLOCK)
            y = _rec_mixer(r, proj, batch, seq, rec_conv_w[r], rec_conv_b[r], w_gate, b_gate, rg_lambda[r],
                           hgrn_lb_logits, hgrn_norm_g[r])
            w_out, o_idx = w_rec_out, r
        else:
            a = layer // 2
            qkv = _qkv_proj(h, g[0], w_qkv_all, a, cos_t, sin_t, seq)
            y = _attention(qkv, batch, seq, att_sinks[a])
            w_out, o_idx = w_att_out, a
        h = _out_mlp(y, w_out, o_idx, g[1], h, g[2], w1_all, w2_all, layer, g[3])
    return h.reshape(batch, seq, d)
```

```python
import functools

import jax
import jax.numpy as jnp
from jax import lax
from jax.experimental import pallas as pl
from jax.experimental.pallas import tpu as pltpu

D_MODEL = 1024
DEPTH = 4
N_REC_LAYERS = (DEPTH + 1) // 2
D_A = D_MODEL // 2
A_HEADS = 4
A_BLOCK = D_A // A_HEADS
CONV_WIDTH = 4
RGLRU_C = 8.0
D_B = D_MODEL // 2
B_HEADS = 4
B_DK = D_B // B_HEADS
HGRN_CHUNK = 32
REC_IN = 2 * D_A + 5 * D_B
HEAD_DIM = 64
N_Q_HEADS = D_MODEL // HEAD_DIM
N_KV_HEADS = 4
GROUP = N_Q_HEADS // N_KV_HEADS
WINDOW = 128
QBLOCK = 128
ROPE_THETA = 10000.0
QKV_OUT = (N_Q_HEADS + 2 * N_KV_HEADS) * HEAD_DIM
D_FF = 4 * D_MODEL
EPS = 1e-6

LANES = 128
SUBLANES = 8
ROW_TILE = 1024
MLP_ROW_TILE = 512
DOT_COLS = 1024
MIX_ROWS = 256
MIX_HEADS = 2
SEG_PITCH = MIX_ROWS + SUBLANES
SCAN_UNROLL = 8
VMEM_LIMIT = 52 * 1024 * 1024

BF16 = jnp.bfloat16
F32 = jnp.float32


def _rms(x, g):
    return x * lax.rsqrt(jnp.mean(x * x, axis=-1, keepdims=True) + EPS) * g


def _dot(a, b):
    return jnp.dot(a, b, preferred_element_type=F32)


def _dot_nt(a, b):
    return lax.dot_general(a, b, (((1,), (1,)), ((), ())), preferred_element_type=F32)


def _dot_tn(a, b):
    return lax.dot_general(a, b, (((0,), (0,)), ((), ())), preferred_element_type=F32)


def _rec_proj_kernel(x_ref, g_ref, w_ref, o_ref):
    xn = _rms(x_ref[...], g_ref[...]).astype(BF16)
    n = o_ref.shape[1]
    for c0 in range(0, n, DOT_COLS):
        c1 = min(c0 + DOT_COLS, n)
        res = _dot(xn, w_ref[:, c0:c1].astype(BF16))
        for t0 in range(c0, c1, LANES):
            tile = res[:, t0 - c0:t0 - c0 + LANES]
            if D_A <= t0 < 2 * D_A:
                tile = _gelu_tanh(tile)
            elif t0 >= REC_IN - D_B:
                tile = tile * _sigmoid(tile)
            o_ref[:, t0:t0 + LANES] = tile.astype(o_ref.dtype)


def _rope(t, cos, sin):
    w = t.shape[1]
    lane = lax.broadcasted_iota(jnp.int32, t.shape, 1)
    rot = jnp.where(lane % HEAD_DIM < HEAD_DIM // 2,
                    pltpu.roll(t, w - HEAD_DIM // 2, 1), pltpu.roll(t, HEAD_DIM // 2, 1))
    return t * cos + rot * sin


def _qkv_proj_kernel(x_ref, g_ref, w_ref, cos_ref, sin_ref, o_ref):
    xn = _rms(x_ref[...], g_ref[...]).astype(BF16)
    q_w = N_Q_HEADS * HEAD_DIM
    rope_w = q_w + N_KV_HEADS * HEAD_DIM
    cos, sin = cos_ref[...], sin_ref[...]
    n = o_ref.shape[1]
    for c0 in range(0, n, DOT_COLS):
        c1 = min(c0 + DOT_COLS, n)
        res = _dot(xn, w_ref[:, c0:c1].astype(BF16))
        for t0 in range(c0, c1, LANES):
            tile = res[:, t0 - c0:t0 - c0 + LANES]
            if t0 < rope_w:
                tile = _rope(tile, cos, sin)
            if t0 < q_w:
                tile = tile * HEAD_DIM ** -0.5
            o_ref[:, t0:t0 + LANES] = tile.astype(o_ref.dtype)


def _qkv_proj(x, g, w_stack, idx, cos_t, sin_t, seq):
    m, k = x.shape
    n = w_stack.shape[2]
    tiles_per_seq = seq // ROW_TILE
    table = pl.BlockSpec((ROW_TILE, LANES), lambda i: (i % tiles_per_seq, 0))
    return pl.pallas_call(
        _qkv_proj_kernel,
        grid=(m // ROW_TILE,),
        in_specs=[
            pl.BlockSpec((ROW_TILE, k), lambda i: (i, 0)),
            pl.BlockSpec((1, k), lambda i: (0, 0)),
            pl.BlockSpec((None, k, n), lambda i: (idx, 0, 0), pipeline_mode=pl.Buffered(1)),
            table, table,
        ],
        out_specs=pl.BlockSpec((ROW_TILE, n), lambda i: (i, 0)),
        out_shape=jax.ShapeDtypeStruct((m, n), BF16),
        compiler_params=pltpu.CompilerParams(
            dimension_semantics=("parallel",), vmem_limit_bytes=VMEM_LIMIT),
        name="qkv_proj",
    )(x, g.reshape(1, k), w_stack, cos_t, sin_t)


def _rec_proj(x, g, w_stack, idx, out_dtype):
    m, k = x.shape
    n = w_stack.shape[2]
    w = w_stack
    return pl.pallas_call(
        _rec_proj_kernel,
        grid=(m // ROW_TILE,),
        in_specs=[
            pl.BlockSpec((ROW_TILE, k), lambda i: (i, 0)),
            pl.BlockSpec((1, k), lambda i: (0, 0)),
            pl.BlockSpec((None, k, n), lambda i: (idx, 0, 0), pipeline_mode=pl.Buffered(1)),
        ],
        out_specs=pl.BlockSpec((ROW_TILE, n), lambda i: (i, 0)),
        out_shape=jax.ShapeDtypeStruct((m, n), out_dtype),
        compiler_params=pltpu.CompilerParams(
            dimension_semantics=("parallel",), vmem_limit_bytes=VMEM_LIMIT),
        name="rec_proj",
    )(x, g.reshape(1, k), w)


def _out_mlp_kernel(a_ref, wo_ref, g_mix_ref, h_ref, g_in_ref, w1_ref, w2_ref, g_out_ref, o_ref):
    h1 = h_ref[...] + _rms(_dot(a_ref[...], wo_ref[...]), g_mix_ref[...])
    xn = _rms(h1, g_in_ref[...]).astype(BF16)
    ff = w1_ref.shape[1]
    acc = None
    for c0 in range(0, ff, DOT_COLS):
        hid = jnp.maximum(_dot(xn, w1_ref[:, c0:c0 + DOT_COLS].astype(BF16)), 0.0)
        part = _dot((hid * hid).astype(BF16), w2_ref[c0:c0 + DOT_COLS, :].astype(BF16))
        acc = part if acc is None else acc + part
    o_ref[...] = h1 + _rms(acc, g_out_ref[...])


def _out_mlp(a, w_o, o_idx, g_mix, h, g_in, w1, w2, layer, g_out):
    m, d = h.shape
    k = a.shape[1]
    ff = w1.shape[2]
    vec = pl.BlockSpec((1, d), lambda i: (0, 0))
    resident = lambda shape, idx: pl.BlockSpec((None,) + shape, lambda i: (idx, 0, 0),
                                               pipeline_mode=pl.Buffered(1))
    return pl.pallas_call(
        _out_mlp_kernel,
        grid=(m // MLP_ROW_TILE,),
        in_specs=[
            pl.BlockSpec((MLP_ROW_TILE, k), lambda i: (i, 0)),
            resident((k, d), o_idx),
            vec,
            pl.BlockSpec((MLP_ROW_TILE, d), lambda i: (i, 0)),
            vec,
            resident((d, ff), layer),
            resident((ff, d), layer),
            vec,
        ],
        out_specs=pl.BlockSpec((MLP_ROW_TILE, d), lambda i: (i, 0)),
        out_shape=jax.ShapeDtypeStruct((m, d), F32),
        compiler_params=pltpu.CompilerParams(
            dimension_semantics=("parallel",), vmem_limit_bytes=VMEM_LIMIT),
        name="out_mlp",
    )(a, w_o, g_mix.reshape(1, d), h, g_in.reshape(1, d), w1, w2, g_out.reshape(1, d))


def _shift_down(x, k):
    return pltpu.roll(x, k, 0)


def _shift_up(x, k):
    return pltpu.roll(x, x.shape[0] - k, 0)


def _chunk_sum_matrix():
    i = jnp.arange(MIX_ROWS)[:, None]
    j = jnp.arange(MIX_ROWS)[None, :]
    same = (i // HGRN_CHUNK) == (j // HGRN_CHUNK)
    return jnp.concatenate([same & (j <= i), same & (j > i)], axis=0).astype(BF16)


def _chunk_causal_masks():
    i = jnp.arange(MIX_ROWS)[:, None]
    j = jnp.arange(MIX_ROWS)[None, :]
    same = (i // HGRN_CHUNK) == (j // HGRN_CHUNK)
    return jnp.stack([same & (j <= i), same & (j >= i)]).astype(F32)


def _sublane_linear_scan(a, u, reverse):
    pos = lax.broadcasted_iota(jnp.int32, a.shape, 0)
    k = 1
    while k < SUBLANES:
        if reverse:
            keep = pos < SUBLANES - k
            a_n, u_n = _shift_up(a, k), _shift_up(u, k)
        else:
            keep = pos >= k
            a_n, u_n = _shift_down(a, k), _shift_down(u, k)
        u = u + jnp.where(keep, a * u_n, 0.0)
        a = a * jnp.where(keep, a_n, 1.0)
        k *= 2
    return u


def _sigmoid(x):
    return 0.5 * jnp.tanh(0.5 * x) + 0.5


def _softplus(x):
    return jnp.maximum(x, 0.0) + jnp.log1p(jnp.exp(-jnp.abs(x)))


def _gelu_tanh(x):
    return 0.5 * x * (1.0 + jnp.tanh(0.7978845608028654 * (x + 0.044715 * (x * x * x))))


def _rec_mixer_kernel(layer, seq,
                      xa_ref, ga_ref, q_ref, zf_ref, zb_ref, iv_ref, g_ref,
                      cw_ref, cb_ref, wg_ref, bg_ref, lam_ref, lbl_ref, ng_ref, tri_ref, mask_ref,
                      y_ref,
                      xpad_ref, a_ref, u_ref, pcum_ref, hloc_ref, cin_ref, o_ref, st_ref):
    n_blocks = seq // MIX_ROWS
    chunks = MIX_ROWS // HGRN_CHUNK
    assert n_blocks == SUBLANES
    zeros8 = jnp.zeros((SUBLANES, LANES), F32)
    ones8 = jnp.ones((SUBLANES, LANES), F32)
    heads = range(MIX_HEADS)
    lane = [pl.ds(hd * LANES, LANES) for hd in heads]

    lbs, sps = [], []
    for hd in heads:
        logits = lbl_ref[:, :, lane[hd]]
        e = jnp.exp(logits - jnp.max(logits, axis=1, keepdims=True))
        s_lb = e / jnp.sum(e, axis=1, keepdims=True)
        lb = jnp.zeros((2, LANES), F32)
        for r in range(1, layer + 1):
            lb = lb + s_lb[:, r, :]
        lbs.append(lb)
        sps.append(_softplus(-lam_ref[:, lane[hd]]))
        xpad_ref[hd, pl.ds(0, SUBLANES), :] = zeros8
        xpad_ref[hd, pl.ds(seq + SUBLANES, SUBLANES), :] = zeros8
        xpad_ref[hd, pl.ds(SUBLANES, seq), :] = xa_ref[:, lane[hd]].astype(F32)
    st_ref[...] = jnp.zeros_like(st_ref)

    def rglru_inputs(hd, c):
        r0 = pl.multiple_of(c * MIX_ROWS, MIX_ROWS)
        seg_rows = pl.ds(pl.multiple_of(c * SEG_PITCH, SUBLANES), MIX_ROWS)
        win = xpad_ref[hd, pl.ds(r0, MIX_ROWS + 2 * SUBLANES), :]
        wrows = MIX_ROWS + 2 * SUBLANES
        xc = jnp.zeros((MIX_ROWS, LANES), F32) + cb_ref[:, lane[hd]]
        for k in range(CONV_WIDTH):
            shift = (2 - k) % wrows
            shifted = (pltpu.roll(win, shift, 0) if shift else win)[SUBLANES:SUBLANES + MIX_ROWS]
            xc = xc + cw_ref[pl.ds(k, 1), lane[hd]] * shifted
        gates = _dot(xc.astype(BF16), wg_ref[hd]) + bg_ref[hd]
        for d in range(2):
            r = _sigmoid(gates[:, (2 * d) * LANES:(2 * d + 1) * LANES])
            i = _sigmoid(gates[:, (2 * d + 1) * LANES:(2 * d + 2) * LANES])
            log_a = (-RGLRU_C) * r * sps[hd][d:d + 1, :]
            a = jnp.exp(log_a)
            a_ref[hd, d, seg_rows, :] = a
            u_ref[hd, d, seg_rows, :] = jnp.sqrt(-jnp.tanh(log_a) * (1.0 + a * a)) * (i * xc)

    def vector_phase(c, carry):
        for hd in heads:
            rglru_inputs(hd, c)
            hgrn_block(hd, c)
        return carry

    def hgrn_block(hd, c):
        lb = lbs[hd]
        rows = [pl.ds(pl.multiple_of(blk * MIX_ROWS, MIX_ROWS), MIX_ROWS) for blk in (c, n_blocks - 1 - c)]
        log_fs, kks, parts = [], [], []
        for d, z_ref in enumerate((zf_ref, zb_ref)):
            sig = _sigmoid(z_ref[rows[d], lane[hd]].astype(F32))
            lbd = lb[d:d + 1, :]
            log_f = jnp.log(lbd + (1.0 - lbd) * sig)
            log_fs.append(log_f)
            kks.append((1.0 - lbd) * (1.0 - sig))
            hi = log_f.astype(BF16)
            parts += [hi, (log_f - hi.astype(F32)).astype(BF16)]
        sums = _dot(tri_ref[...], jnp.concatenate(parts, axis=1))
        pre_f = sums[:MIX_ROWS, 0:LANES] + sums[:MIX_ROWS, LANES:2 * LANES]
        suf_f = sums[MIX_ROWS:, 0:LANES] + sums[MIX_ROWS:, LANES:2 * LANES]
        pre_b = sums[:MIX_ROWS, 2 * LANES:3 * LANES] + sums[:MIX_ROWS, 3 * LANES:]
        suf_b = sums[MIX_ROWS:, 2 * LANES:3 * LANES] + sums[MIX_ROWS:, 3 * LANES:]
        bcums = (pre_f, suf_b + log_fs[1])
        rests = (suf_f, pre_b - log_fs[1])
        chunk_slices = [slice(j * HGRN_CHUNK, (j + 1) * HGRN_CHUNK) for j in range(chunks)]
        qes, decs, o_ins, upds = [], [], [], []
        for d in range(2):
            bcum, rest, kk = bcums[d], rests[d], kks[d]
            vv = iv_ref[rows[d], lane[hd]].astype(BF16)
            qe = (q_ref[rows[d], lane[hd]].astype(F32) * jnp.exp(bcum)).astype(BF16)
            ke = (kk * jnp.exp(-bcum)).astype(BF16)
            kd = (kk * jnp.exp(rest)).astype(BF16)
            att = jnp.where(mask_ref[d] > 0.5, _dot_nt(qe, ke), 0.0)
            qes.append(qe)
            o_ins.append(_dot(att.astype(BF16), vv))
            decs.append([jnp.exp(bcum[ch.start:ch.start + 1, :] + rest[ch.start:ch.start + 1, :])
                         for ch in chunk_slices])
            upds.append([_dot_tn(vv[ch], kd[ch]) for ch in chunk_slices])
        entering = []
        for d in range(2):
            st = st_ref[hd, d]
            states = [None] * chunks
            for j in (range(chunks) if d == 0 else reversed(range(chunks))):
                states[j] = st.astype(BF16)
                st = decs[d][j] * st + upds[d][j]
            st_ref[hd, d] = st
            entering.append(states)
        for d in range(2):
            pieces = [_dot_nt(qes[d][ch], entering[d][j]) for j, ch in enumerate(chunk_slices)]
            o_ref[hd, d, rows[d], :] = o_ins[d] + jnp.concatenate(pieces, axis=0)

    lax.fori_loop(0, n_blocks, vector_phase, 0)

    def scan_phase(t, carry):
        out = []
        for hd in heads:
            hf, pf, hb, pb = carry[4 * hd:4 * hd + 4]
            tb = MIX_ROWS - 1 - t
            af = a_ref[hd, 0, pl.ds(t, SUBLANES, stride=SEG_PITCH), :]
            hf = af * hf + u_ref[hd, 0, pl.ds(t, SUBLANES, stride=SEG_PITCH), :]
            pf = pf * af
            hloc_ref[hd, 0, pl.ds(t, SUBLANES, stride=SEG_PITCH), :] = hf
            pcum_ref[hd, 0, pl.ds(t, SUBLANES, stride=SEG_PITCH), :] = pf
            ab = a_ref[hd, 1, pl.ds(tb, SUBLANES, stride=SEG_PITCH), :]
            hb = ab * hb + u_ref[hd, 1, pl.ds(tb, SUBLANES, stride=SEG_PITCH), :]
            pb = pb * ab
            hloc_ref[hd, 1, pl.ds(tb, SUBLANES, stride=SEG_PITCH), :] = hb
            pcum_ref[hd, 1, pl.ds(tb, SUBLANES, stride=SEG_PITCH), :] = pb
            out += [hf, pf, hb, pb]
        return tuple(out)

    totals = lax.fori_loop(0, MIX_ROWS, scan_phase, (zeros8, ones8, zeros8, ones8) * MIX_HEADS,
                           unroll=SCAN_UNROLL)
    sub = lax.broadcasted_iota(jnp.int32, (SUBLANES, LANES), 0)
    for hd in heads:
        hf, pf, hb, pb = totals[4 * hd:4 * hd + 4]
        cin_ref[hd, 0] = jnp.where(sub >= 1, _shift_down(_sublane_linear_scan(pf, hf, False), 1), 0.0)
        cin_ref[hd, 1] = jnp.where(sub < SUBLANES - 1, _shift_up(_sublane_linear_scan(pb, hb, True), 1), 0.0)

    def out_phase(c, carry):
        rows = pl.ds(pl.multiple_of(c * MIX_ROWS, MIX_ROWS), MIX_ROWS)
        seg_rows = pl.ds(pl.multiple_of(c * SEG_PITCH, SUBLANES), MIX_ROWS)
        for hd in heads:
            h = (hloc_ref[hd, 0, seg_rows, :] + pcum_ref[hd, 0, seg_rows, :] * cin_ref[hd, 0, pl.ds(c, 1), :]
                 + hloc_ref[hd, 1, seg_rows, :] + pcum_ref[hd, 1, seg_rows, :] * cin_ref[hd, 1, pl.ds(c, 1), :])
            y_a = ga_ref[rows, lane[hd]].astype(F32) * h
            y_b = (_rms(o_ref[hd, 0, rows, :] + o_ref[hd, 1, rows, :], ng_ref[:, lane[hd]])
                   * g_ref[rows, lane[hd]].astype(F32))
            y_ref[rows, pl.ds(2 * hd * LANES, LANES)] = y_a.astype(y_ref.dtype)
            y_ref[rows, pl.ds((2 * hd + 1) * LANES, LANES)] = y_b.astype(y_ref.dtype)
        return carry

    lax.fori_loop(0, n_blocks, out_phase, 0)


def _rec_mixer(layer, proj, batch, seq, conv_w, conv_b, w_gate, b_gate, lam, lb_logits, hg_norm):
    nh = A_HEADS
    hs = MIX_HEADS
    w = hs * LANES
    groups = nh // hs
    col = lambda part: pl.BlockSpec((seq, w), lambda b, h, part=part: (b, part * groups + h))
    in_specs = [col(0), col(1), col(2), col(3), col(4), col(5), col(6),
                pl.BlockSpec((CONV_WIDTH, w), lambda b, h: (0, h)),
                pl.BlockSpec((1, w), lambda b, h: (0, h)),
                pl.BlockSpec((hs, LANES, 4 * LANES), lambda b, h: (h, 0, 0)),
                pl.BlockSpec((hs, 1, 4 * LANES), lambda b, h: (h, 0, 0)),
                pl.BlockSpec((2, w), lambda b, h: (0, h)),
                pl.BlockSpec((2, N_REC_LAYERS, w), lambda b, h: (0, 0, h)),
                pl.BlockSpec((1, w), lambda b, h: (0, h)),
                pl.BlockSpec((2 * MIX_ROWS, MIX_ROWS), lambda b, h: (0, 0)),
                pl.BlockSpec((2, MIX_ROWS, MIX_ROWS), lambda b, h: (0, 0, 0))]
    seg_total = (seq // MIX_ROWS) * SEG_PITCH
    return pl.pallas_call(
        functools.partial(_rec_mixer_kernel, layer, seq),
        grid=(batch, groups),
        in_specs=in_specs,
        out_specs=pl.BlockSpec((seq, 2 * w), lambda b, h: (b, h)),
        out_shape=jax.ShapeDtypeStruct((batch * seq, D_A + D_B), BF16),
        scratch_shapes=[
            pltpu.VMEM((hs, seq + 2 * SUBLANES, LANES), F32),
            pltpu.VMEM((hs, 2, seg_total, LANES), F32),
            pltpu.VMEM((hs, 2, seg_total, LANES), F32),
            pltpu.VMEM((hs, 2, seg_total, LANES), F32),
            pltpu.VMEM((hs, 2, seg_total, LANES), F32),
            pltpu.VMEM((hs, 2, SUBLANES, LANES), F32),
            pltpu.VMEM((hs, 2, seq, LANES), F32),
            pltpu.VMEM((hs, 2, LANES, LANES), F32),
        ],
        compiler_params=pltpu.CompilerParams(
            dimension_semantics=("parallel", "parallel"), vmem_limit_bytes=VMEM_LIMIT),
        name="rec_mixer",
    )(proj, proj, proj, proj, proj, proj, proj,
      conv_w, conv_b.reshape(1, D_A), w_gate, b_gate, lam, lb_logits, hg_norm.reshape(1, D_B),
      _chunk_sum_matrix(), _chunk_causal_masks())


ATTN_QBLOCKS = 8
PREP_UNROLL = 4
VT_PAD = 16


def _attn_kernel(seq, q_ref, k_ref, v_ref, sink_ref, o_ref, ks_ref, vt_ref):
    n = pl.program_id(1)
    span = 3 * QBLOCK
    heads_per_tile = LANES // HEAD_DIM
    tiles_per_group = GROUP // heads_per_tile

    @pl.when(n == 0)
    def _():
        def prep(j, carry):
            r = pl.ds(pl.multiple_of(j * QBLOCK, QBLOCK), QBLOCK)
            upper = lax.broadcasted_iota(jnp.int32, (QBLOCK, LANES), 1) >= HEAD_DIM
            ones_tile = jnp.where(lax.broadcasted_iota(jnp.int32, (VT_PAD, QBLOCK), 0) == 0, 1.0, 0.0)
            for t in range(N_KV_HEADS // heads_per_tile):
                kt = k_ref[r, pl.ds(t * LANES, LANES)].astype(F32)
                vt = v_ref[r, pl.ds(t * LANES, LANES)].astype(F32).T
                for hh in range(heads_per_tile):
                    g = t * heads_per_tile + hh
                    k_own = jnp.where(upper if hh else jnp.logical_not(upper), kt, 0.0)
                    ks_ref[g, r, :] = (k_own + pltpu.roll(k_own, HEAD_DIM, 1)).astype(BF16)
                    vt_ref[g, j, pl.ds(0, HEAD_DIM), :] = vt[hh * HEAD_DIM:(hh + 1) * HEAD_DIM].astype(BF16)
                    vt_ref[g, j, pl.ds(HEAD_DIM, VT_PAD), :] = ones_tile.astype(BF16)
            return carry

        lax.fori_loop(0, seq // QBLOCK, prep, 0, unroll=PREP_UNROLL)

    cols = GROUP * QBLOCK
    col = lax.broadcasted_iota(jnp.int32, (1, cols), 1)
    upper = lax.broadcasted_iota(jnp.int32, (tiles_per_group * QBLOCK, LANES), 1) >= HEAD_DIM
    subs = range(ATTN_QBLOCKS)
    starts, biases = [], []
    for s in subs:
        q0 = pl.multiple_of((n * ATTN_QBLOCKS + s) * QBLOCK, QBLOCK)
        start = pl.multiple_of(jnp.clip(q0 - QBLOCK, 0, seq - span), QBLOCK)
        kpos = start + lax.broadcasted_iota(jnp.int32, (span, LANES), 0)
        qpos = q0 + lax.broadcasted_iota(jnp.int32, (span, LANES), 1)
        bias = jnp.where(jnp.abs(qpos - kpos) <= WINDOW, 0.0, -jnp.inf)
        starts.append(start)
        biases.append(jnp.concatenate([bias] * GROUP, axis=1))
    scores, weights = {}, {}
    for s in subs:
        q_rows = pl.ds(s * QBLOCK, QBLOCK)
        for g in range(N_KV_HEADS):
            qs = jnp.concatenate([q_ref[q_rows, pl.ds((g * tiles_per_group + t) * LANES, LANES)]
                                  for t in range(tiles_per_group)], axis=0)
            zero = jnp.zeros_like(qs)
            rhs = jnp.concatenate([jnp.where(upper, zero, qs), jnp.where(upper, qs, zero)], axis=0)
            sc = _dot_nt(ks_ref[g, pl.ds(starts[s], span), :], rhs) + biases[s]
            sink = sink_ref[g * GROUP]
            for p in range(heads_per_tile):
                for t in range(tiles_per_group):
                    if p or t:
                        sink = jnp.where(col >= (p * tiles_per_group + t) * QBLOCK,
                                         sink_ref[g * GROUP + t * heads_per_tile + p], sink)
            m = jnp.maximum(jnp.max(sc, axis=0, keepdims=True), sink)
            scores[s, g] = (sc, m, jnp.exp(sink - m))
    for s in subs:
        for g in range(N_KV_HEADS):
            sc, m, e_sink = scores[s, g]
            weights[s, g] = (jnp.exp((sc - m).astype(BF16)), e_sink)
    for s in subs:
        first_block = starts[s] // QBLOCK
        for g in range(N_KV_HEADS):
            e, e_sink = weights[s, g]
            pv = jnp.zeros((HEAD_DIM + VT_PAD, cols), F32)
            for j in range(span // QBLOCK):
                pv = pv + _dot(vt_ref[g, first_block + j], e[j * QBLOCK:(j + 1) * QBLOCK, :])
            out = pv[:HEAD_DIM] * (1.0 / (pv[HEAD_DIM:HEAD_DIM + 1] + e_sink))
            for t in range(tiles_per_group):
                tile = jnp.concatenate(
                    [out[:, (p * tiles_per_group + t) * QBLOCK:(p * tiles_per_group + t + 1) * QBLOCK]
                     for p in range(heads_per_tile)], axis=0)
                o_ref[pl.ds(s * QBLOCK, QBLOCK), pl.ds((g * tiles_per_group + t) * LANES, LANES)] = (
                    tile.T.astype(o_ref.dtype))


def _attention(qkv, batch, seq, sinks):
    nb = seq // QBLOCK
    steps = nb // ATTN_QBLOCKS
    q_w = N_Q_HEADS * HEAD_DIM
    kv_w = N_KV_HEADS * HEAD_DIM
    return pl.pallas_call(
        functools.partial(_attn_kernel, seq),
        grid=(batch, steps),
        in_specs=[
            pl.BlockSpec((ATTN_QBLOCKS * QBLOCK, q_w), lambda b, n: (b * steps + n, 0)),
            pl.BlockSpec((seq, kv_w), lambda b, n: (b, q_w // kv_w)),
            pl.BlockSpec((seq, kv_w), lambda b, n: (b, q_w // kv_w + 1)),
            pl.BlockSpec(memory_space=pltpu.SMEM),
        ],
        out_specs=pl.BlockSpec((ATTN_QBLOCKS * QBLOCK, q_w), lambda b, n: (b * steps + n, 0)),
        out_shape=jax.ShapeDtypeStruct((batch * seq, q_w), BF16),
        scratch_shapes=[pltpu.VMEM((N_KV_HEADS, seq, LANES), BF16),
                        pltpu.VMEM((N_KV_HEADS, nb, HEAD_DIM + VT_PAD, QBLOCK), BF16)],
        compiler_params=pltpu.CompilerParams(
            dimension_semantics=("parallel", "arbitrary"), vmem_limit_bytes=VMEM_LIMIT),
        name="window_attention",
    )(qkv, qkv, qkv, sinks)


def _rope_tables(seq):
    pos = jnp.arange(seq, dtype=F32)
    inv_freq = ROPE_THETA ** (-jnp.arange(0, HEAD_DIM, 2, dtype=F32) / HEAD_DIM)
    ang = pos[:, None] * inv_freq[None, :]
    cos, sin = jnp.cos(ang), jnp.sin(ang)
    reps = LANES // HEAD_DIM
    return (jnp.tile(jnp.concatenate([cos, cos], axis=1), (1, reps)),
            jnp.tile(jnp.concatenate([-sin, sin], axis=1), (1, reps)))


def kernel(x, norm_g, rec_w_in, rec_conv_w, rec_conv_b, rg_w_r, rg_b_r, rg_w_i, rg_b_i, rg_lambda,
           hgrn_lb_logits, hgrn_norm_g, rec_w_out, att_w_qkv, att_sinks, att_w_o, mlp_w1, mlp_w2):
    batch, seq, d = x.shape
    cos_t, sin_t = _rope_tables(seq)
    h = x.reshape(batch * seq, d)
    w_att_out = att_w_o.astype(BF16)
    w_in_all, w_qkv_all = rec_w_in, att_w_qkv
    w1_all, w2_all = mlp_w1, mlp_w2
    w_rec_out = (rec_w_out.reshape(N_REC_LAYERS, 2, A_HEADS, A_BLOCK, d).transpose(0, 2, 1, 3, 4)
                 .reshape(N_REC_LAYERS, D_A + D_B, d).astype(BF16))
    for layer in range(DEPTH):
        g = norm_g[layer]
        if layer % 2 == 0:
            r = layer // 2
            proj = _rec_proj(h, g[0], w_in_all, r, BF16)
            w_gate = jnp.concatenate([rg_w_r[r, 0], rg_w_i[r, 0], rg_w_r[r, 1], rg_w_i[r, 1]], axis=-1).astype(BF16)
            b_gate = jnp.stack([rg_b_r[r, 0], rg_b_i[r, 0], rg_b_r[r, 1], rg_b_i[r, 1]], axis=0)
            b_gate = b_gate.reshape(4, A_HEADS, A_BLOCK).transpose(1, 0, 2).reshape(A_HEADS, 1, 4 * A_BLOCK)
            y = _rec_mixer(r, proj, batch, seq, rec_conv_w[r], rec_conv_b[r], w_gate, b_gate, rg_lambda[r],
                           hgrn_lb_logits, hgrn_norm_g[r])
            w_out, o_idx = w_rec_out, r
        else:
            a = layer // 2
            qkv = _qkv_proj(h, g[0], w_qkv_all, a, cos_t, sin_t, seq)
            y = _attention(qkv, batch, seq, att_sinks[a])
            w_out, o_idx = w_att_out, a
        h = _out_mlp(y, w_out, o_idx, g[1], h, g[2], w1_all, w2_all, layer, g[3])
    return h.reshape(batch, seq, d)
```

```python
import functools

import jax
import jax.numpy as jnp
from jax import lax
from jax.experimental import pallas as pl
from jax.experimental.pallas import tpu as pltpu

D_MODEL = 1024
DEPTH = 4
N_REC_LAYERS = (DEPTH + 1) // 2
D_A = D_MODEL // 2
A_HEADS = 4
A_BLOCK = D_A // A_HEADS
CONV_WIDTH = 4
RGLRU_C = 8.0
D_B = D_MODEL // 2
B_HEADS = 4
B_DK = D_B // B_HEADS
HGRN_CHUNK = 32
REC_IN = 2 * D_A + 5 * D_B
HEAD_DIM = 64
N_Q_HEADS = D_MODEL // HEAD_DIM
N_KV_HEADS = 4
GROUP = N_Q_HEADS // N_KV_HEADS
WINDOW = 128
QBLOCK = 128
ROPE_THETA = 10000.0
QKV_OUT = (N_Q_HEADS + 2 * N_KV_HEADS) * HEAD_DIM
D_FF = 4 * D_MODEL
EPS = 1e-6

LANES = 128
SUBLANES = 8
ROW_TILE = 1024
MLP_ROW_TILE = 512
DOT_COLS = 1024
MIX_ROWS = 256
MIX_HEADS = 2
SEG_PITCH = MIX_ROWS + SUBLANES
SCAN_UNROLL = 8
VMEM_LIMIT = 52 * 1024 * 1024

BF16 = jnp.bfloat16
F32 = jnp.float32


def _rms(x, g):
    return x * lax.rsqrt(jnp.mean(x * x, axis=-1, keepdims=True) + EPS) * g


def _dot(a, b):
    return jnp.dot(a, b, preferred_element_type=F32)


def _dot_nt(a, b):
    return lax.dot_general(a, b, (((1,), (1,)), ((), ())), preferred_element_type=F32)


def _dot_tn(a, b):
    return lax.dot_general(a, b, (((0,), (0,)), ((), ())), preferred_element_type=F32)


def _rec_proj_kernel(x_ref, g_ref, w_ref, o_ref):
    xn = _rms(x_ref[...], g_ref[...]).astype(BF16)
    n = o_ref.shape[1]
    for c0 in range(0, n, DOT_COLS):
        c1 = min(c0 + DOT_COLS, n)
        res = _dot(xn, w_ref[:, c0:c1].astype(BF16))
        for t0 in range(c0, c1, LANES):
            tile = res[:, t0 - c0:t0 - c0 + LANES]
            if D_A <= t0 < 2 * D_A:
                tile = _gelu_tanh(tile)
            elif t0 >= REC_IN - D_B:
                tile = tile * _sigmoid(tile)
            o_ref[:, t0:t0 + LANES] = tile.astype(o_ref.dtype)


def _rope(t, cos, sin):
    w = t.shape[1]
    lane = lax.broadcasted_iota(jnp.int32, t.shape, 1)
    rot = jnp.where(lane % HEAD_DIM < HEAD_DIM // 2,
                    pltpu.roll(t, w - HEAD_DIM // 2, 1), pltpu.roll(t, HEAD_DIM // 2, 1))
    return t * cos + rot * sin


def _qkv_proj_kernel(x_ref, g_ref, w_ref, cos_ref, sin_ref, o_ref):
    xn = _rms(x_ref[...], g_ref[...]).astype(BF16)
    q_w = N_Q_HEADS * HEAD_DIM
    rope_w = q_w + N_KV_HEADS * HEAD_DIM
    cos, sin = cos_ref[...], sin_ref[...]
    n = o_ref.shape[1]
    for c0 in range(0, n, DOT_COLS):
        c1 = min(c0 + DOT_COLS, n)
        res = _dot(xn, w_ref[:, c0:c1].astype(BF16))
        for t0 in range(c0, c1, LANES):
            tile = res[:, t0 - c0:t0 - c0 + LANES]
            if t0 < rope_w:
                tile = _rope(tile, cos, sin)
            if t0 < q_w:
                tile = tile * HEAD_DIM ** -0.5
            o_ref[:, t0:t0 + LANES] = tile.astype(o_ref.dtype)


def _qkv_proj(x, g, w_stack, idx, cos_t, sin_t, seq):
    m, k = x.shape
    n = w_stack.shape[2]
    tiles_per_seq = seq // ROW_TILE
    table = pl.BlockSpec((ROW_TILE, LANES), lambda i: (i % tiles_per_seq, 0))
    return pl.pallas_call(
        _qkv_proj_kernel,
        grid=(m // ROW_TILE,),
        in_specs=[
            pl.BlockSpec((ROW_TILE, k), lambda i: (i, 0)),
            pl.BlockSpec((1, k), lambda i: (0, 0)),
            pl.BlockSpec((None, k, n), lambda i: (idx, 0, 0), pipeline_mode=pl.Buffered(1)),
            table, table,
        ],
        out_specs=pl.BlockSpec((ROW_TILE, n), lambda i: (i, 0)),
        out_shape=jax.ShapeDtypeStruct((m, n), BF16),
        compiler_params=pltpu.CompilerParams(
            dimension_semantics=("parallel",), vmem_limit_bytes=VMEM_LIMIT),
        name="qkv_proj",
    )(x, g.reshape(1, k), w_stack, cos_t, sin_t)


def _rec_proj(x, g, w_stack, idx, out_dtype):
    m, k = x.shape
    n = w_stack.shape[2]
    w = w_stack
    return pl.pallas_call(
        _rec_proj_kernel,
        grid=(m // ROW_TILE,),
        in_specs=[
            pl.BlockSpec((ROW_TILE, k), lambda i: (i, 0)),
            pl.BlockSpec((1, k), lambda i: (0, 0)),
            pl.BlockSpec((None, k, n), lambda i: (idx, 0, 0), pipeline_mode=pl.Buffered(1)),
        ],
        out_specs=pl.BlockSpec((ROW_TILE, n), lambda i: (i, 0)),
        out_shape=jax.ShapeDtypeStruct((m, n), out_dtype),
        compiler_params=pltpu.CompilerParams(
            dimension_semantics=("parallel",), vmem_limit_bytes=VMEM_LIMIT),
        name="rec_proj",
    )(x, g.reshape(1, k), w)


def _out_mlp_kernel(a_ref, wo_ref, g_mix_ref, h_ref, g_in_ref, w1_ref, w2_ref, g_out_ref, o_ref):
    h1 = h_ref[...] + _rms(_dot(a_ref[...], wo_ref[...]), g_mix_ref[...])
    xn = _rms(h1, g_in_ref[...]).astype(BF16)
    ff = w1_ref.shape[1]
    acc = None
    for c0 in range(0, ff, DOT_COLS):
        hid = jnp.maximum(_dot(xn, w1_ref[:, c0:c0 + DOT_COLS].astype(BF16)), 0.0)
        part = _dot((hid * hid).astype(BF16), w2_ref[c0:c0 + DOT_COLS, :].astype(BF16))
        acc = part if acc is None else acc + part
    o_ref[...] = h1 + _rms(acc, g_out_ref[...])


def _out_mlp(a, w_o, o_idx, g_mix, h, g_in, w1, w2, layer, g_out):
    m, d = h.shape
    k = a.shape[1]
    ff = w1.shape[2]
    vec = pl.BlockSpec((1, d), lambda i: (0, 0))
    resident = lambda shape, idx: pl.BlockSpec((None,) + shape, lambda i: (idx, 0, 0),
                                               pipeline_mode=pl.Buffered(1))
    return pl.pallas_call(
        _out_mlp_kernel,
        grid=(m // MLP_ROW_TILE,),
        in_specs=[
            pl.BlockSpec((MLP_ROW_TILE, k), lambda i: (i, 0)),
            resident((k, d), o_idx),
            vec,
            pl.BlockSpec((MLP_ROW_TILE, d), lambda i: (i, 0)),
            vec,
            resident((d, ff), layer),
            resident((ff, d), layer),
            vec,
        ],
        out_specs=pl.BlockSpec((MLP_ROW_TILE, d), lambda i: (i, 0)),
        out_shape=jax.ShapeDtypeStruct((m, d), F32),
        compiler_params=pltpu.CompilerParams(
            dimension_semantics=("parallel",), vmem_limit_bytes=VMEM_LIMIT),
        name="out_mlp",
    )(a, w_o, g_mix.reshape(1, d), h, g_in.reshape(1, d), w1, w2, g_out.reshape(1, d))


def _shift_down(x, k):
    return pltpu.roll(x, k, 0)


def _shift_up(x, k):
    return pltpu.roll(x, x.shape[0] - k, 0)


def _chunk_sum_matrix():
    i = jnp.arange(MIX_ROWS)[:, None]
    j = jnp.arange(MIX_ROWS)[None, :]
    same = (i // HGRN_CHUNK) == (j // HGRN_CHUNK)
    return jnp.concatenate([same & (j <= i), same & (j > i)], axis=0).astype(BF16)


def _chunk_causal_masks():
    i = jnp.arange(MIX_ROWS)[:, None]
    j = jnp.arange(MIX_ROWS)[None, :]
    same = (i // HGRN_CHUNK) == (j // HGRN_CHUNK)
    return jnp.stack([same & (j <= i), same & (j >= i)]).astype(F32)


def _sublane_linear_scan(a, u, reverse):
    pos = lax.broadcasted_iota(jnp.int32, a.shape, 0)
    k = 1
    while k < SUBLANES:
        if reverse:
            keep = pos < SUBLANES - k
            a_n, u_n = _shift_up(a, k), _shift_up(u, k)
        else:
            keep = pos >= k
            a_n, u_n = _shift_down(a, k), _shift_down(u, k)
        u = u + jnp.where(keep, a * u_n, 0.0)
        a = a * jnp.where(keep, a_n, 1.0)
        k *= 2
    return u


def _sigmoid(x):
    return 0.5 * jnp.tanh(0.5 * x) + 0.5


def _softplus(x):
    return jnp.maximum(x, 0.0) + jnp.log1p(jnp.exp(-jnp.abs(x)))


def _gelu_tanh(x):
    return 0.5 * x * (1.0 + jnp.tanh(0.7978845608028654 * (x + 0.044715 * (x * x * x))))


def _rec_mixer_kernel(layer, seq,
                      xa_ref, ga_ref, q_ref, zf_ref, zb_ref, iv_ref, g_ref,
                      cw_ref, cb_ref, wg_ref, bg_ref, lam_ref, lbl_ref, ng_ref, tri_ref, mask_ref,
                      y_ref,
                      xpad_ref, a_ref, u_ref, pcum_ref, hloc_ref, cin_ref, o_ref, st_ref):
    n_blocks = seq // MIX_ROWS
    chunks = MIX_ROWS // HGRN_CHUNK
    assert n_blocks == SUBLANES
    zeros8 = jnp.zeros((SUBLANES, LANES), F32)
    ones8 = jnp.ones((SUBLANES, LANES), F32)
    heads = range(MIX_HEADS)
    lane = [pl.ds(hd * LANES, LANES) for hd in heads]

    lbs, sps = [], []
    for hd in heads:
        logits = lbl_ref[:, :, lane[hd]]
        e = jnp.exp(logits - jnp.max(logits, axis=1, keepdims=True))
        s_lb = e / jnp.sum(e, axis=1, keepdims=True)
        lb = jnp.zeros((2, LANES), F32)
        for r in range(1, layer + 1):
            lb = lb + s_lb[:, r, :]
        lbs.append(lb)
        sps.append(_softplus(-lam_ref[:, lane[hd]]))
        xpad_ref[hd, pl.ds(0, SUBLANES), :] = zeros8
        xpad_ref[hd, pl.ds(seq + SUBLANES, SUBLANES), :] = zeros8
        xpad_ref[hd, pl.ds(SUBLANES, seq), :] = xa_ref[:, lane[hd]].astype(F32)
    st_ref[...] = jnp.zeros_like(st_ref)

    def rglru_inputs(hd, c):
        r0 = pl.multiple_of(c * MIX_ROWS, MIX_ROWS)
        seg_rows = pl.ds(pl.multiple_of(c * SEG_PITCH, SUBLANES), MIX_ROWS)
        win = xpad_ref[hd, pl.ds(r0, MIX_ROWS + 2 * SUBLANES), :]
        wrows = MIX_ROWS + 2 * SUBLANES
        xc = jnp.zeros((MIX_ROWS, LANES), F32) + cb_ref[:, lane[hd]]
        for k in range(CONV_WIDTH):
            shift = (2 - k) % wrows
            shifted = (pltpu.roll(win, shift, 0) if shift else win)[SUBLANES:SUBLANES + MIX_ROWS]
            xc = xc + cw_ref[pl.ds(k, 1), lane[hd]] * shifted
        gates = _dot(xc.astype(BF16), wg_ref[hd]) + bg_ref[hd]
        for d in range(2):
            r = _sigmoid(gates[:, (2 * d) * LANES:(2 * d + 1) * LANES])
            i = _sigmoid(gates[:, (2 * d + 1) * LANES:(2 * d + 2) * LANES])
            log_a = (-RGLRU_C) * r * sps[hd][d:d + 1, :]
            a = jnp.exp(log_a)
            a_ref[hd, d, seg_rows, :] = a
            u_ref[hd, d, seg_rows, :] = jnp.sqrt(-jnp.tanh(log_a) * (1.0 + a * a)) * (i * xc)

    def vector_phase(c, carry):
        for hd in heads:
            rglru_inputs(hd, c)
            hgrn_block(hd, c)
        return carry

    def hgrn_block(hd, c):
        lb = lbs[hd]
        rows = [pl.ds(pl.multiple_of(blk * MIX_ROWS, MIX_ROWS), MIX_ROWS) for blk in (c, n_blocks - 1 - c)]
        log_fs, kks, parts = [], [], []
        for d, z_ref in enumerate((zf_ref, zb_ref)):
            sig = jax.nn.sigmoid(z_ref[rows[d], lane[hd]].astype(F32))
            lbd = lb[d:d + 1, :]
            log_f = jnp.log(lbd + (1.0 - lbd) * sig)
            log_fs.append(log_f)
            kks.append((1.0 - lbd) * (1.0 - sig))
            hi = log_f.astype(BF16)
            parts += [hi, (log_f - hi.astype(F32)).astype(BF16)]
        sums = _dot(tri_ref[...], jnp.concatenate(parts, axis=1))
        pre_f = sums[:MIX_ROWS, 0:LANES] + sums[:MIX_ROWS, LANES:2 * LANES]
        suf_f = sums[MIX_ROWS:, 0:LANES] + sums[MIX_ROWS:, LANES:2 * LANES]
        pre_b = sums[:MIX_ROWS, 2 * LANES:3 * LANES] + sums[:MIX_ROWS, 3 * LANES:]
        suf_b = sums[MIX_ROWS:, 2 * LANES:3 * LANES] + sums[MIX_ROWS:, 3 * LANES:]
        bcums = (pre_f, suf_b + log_fs[1])
        rests = (suf_f, pre_b - log_fs[1])
        chunk_slices = [slice(j * HGRN_CHUNK, (j + 1) * HGRN_CHUNK) for j in range(chunks)]
        qes, decs, o_ins, upds = [], [], [], []
        for d in range(2):
            bcum, rest, kk = bcums[d], rests[d], kks[d]
            vv = iv_ref[rows[d], lane[hd]].astype(BF16)
            qe = (q_ref[rows[d], lane[hd]].astype(F32) * jnp.exp(bcum)).astype(BF16)
            ke = (kk * jnp.exp(-bcum)).astype(BF16)
            kd = (kk * jnp.exp(rest)).astype(BF16)
            att = jnp.where(mask_ref[d] > 0.5, _dot_nt(qe, ke), 0.0)
            qes.append(qe)
            o_ins.append(_dot(att.astype(BF16), vv))
            decs.append([jnp.exp(bcum[ch.start:ch.start + 1, :] + rest[ch.start:ch.start + 1, :])
                         for ch in chunk_slices])
            upds.append([_dot_tn(vv[ch], kd[ch]) for ch in chunk_slices])
        entering = []
        for d in range(2):
            st = st_ref[hd, d]
            states = [None] * chunks
            for j in (range(chunks) if d == 0 else reversed(range(chunks))):
                states[j] = st.astype(BF16)
                st = decs[d][j] * st + upds[d][j]
            st_ref[hd, d] = st
            entering.append(states)
        for d in range(2):
            pieces = [_dot_nt(qes[d][ch], entering[d][j]) for j, ch in enumerate(chunk_slices)]
            o_ref[hd, d, rows[d], :] = o_ins[d] + jnp.concatenate(pieces, axis=0)

    lax.fori_loop(0, n_blocks, vector_phase, 0)

    def scan_phase(t, carry):
        out = []
        for hd in heads:
            hf, pf, hb, pb = carry[4 * hd:4 * hd + 4]
            tb = MIX_ROWS - 1 - t
            af = a_ref[hd, 0, pl.ds(t, SUBLANES, stride=SEG_PITCH), :]
            hf = af * hf + u_ref[hd, 0, pl.ds(t, SUBLANES, stride=SEG_PITCH), :]
            pf = pf * af
            hloc_ref[hd, 0, pl.ds(t, SUBLANES, stride=SEG_PITCH), :] = hf
            pcum_ref[hd, 0, pl.ds(t, SUBLANES, stride=SEG_PITCH), :] = pf
            ab = a_ref[hd, 1, pl.ds(tb, SUBLANES, stride=SEG_PITCH), :]
            hb = ab * hb + u_ref[hd, 1, pl.ds(tb, SUBLANES, stride=SEG_PITCH), :]
            pb = pb * ab
            hloc_ref[hd, 1, pl.ds(tb, SUBLANES, stride=SEG_PITCH), :] = hb
            pcum_ref[hd, 1, pl.ds(tb, SUBLANES, stride=SEG_PITCH), :] = pb
            out += [hf, pf, hb, pb]
        return tuple(out)

    totals = lax.fori_loop(0, MIX_ROWS, scan_phase, (zeros8, ones8, zeros8, ones8) * MIX_HEADS,
                           unroll=SCAN_UNROLL)
    sub = lax.broadcasted_iota(jnp.int32, (SUBLANES, LANES), 0)
    for hd in heads:
        hf, pf, hb, pb = totals[4 * hd:4 * hd + 4]
        cin_ref[hd, 0] = jnp.where(sub >= 1, _shift_down(_sublane_linear_scan(pf, hf, False), 1), 0.0)
        cin_ref[hd, 1] = jnp.where(sub < SUBLANES - 1, _shift_up(_sublane_linear_scan(pb, hb, True), 1), 0.0)

    def out_phase(c, carry):
        rows = pl.ds(pl.multiple_of(c * MIX_ROWS, MIX_ROWS), MIX_ROWS)
        seg_rows = pl.ds(pl.multiple_of(c * SEG_PITCH, SUBLANES), MIX_ROWS)
        for hd in heads:
            h = (hloc_ref[hd, 0, seg_rows, :] + pcum_ref[hd, 0, seg_rows, :] * cin_ref[hd, 0, pl.ds(c, 1), :]
                 + hloc_ref[hd, 1, seg_rows, :] + pcum_ref[hd, 1, seg_rows, :] * cin_ref[hd, 1, pl.ds(c, 1), :])
            y_a = ga_ref[rows, lane[hd]].astype(F32) * h
            y_b = (_rms(o_ref[hd, 0, rows, :] + o_ref[hd, 1, rows, :], ng_ref[:, lane[hd]])
                   * g_ref[rows, lane[hd]].astype(F32))
            y_ref[rows, pl.ds(2 * hd * LANES, LANES)] = y_a.astype(y_ref.dtype)
            y_ref[rows, pl.ds((2 * hd + 1) * LANES, LANES)] = y_b.astype(y_ref.dtype)
        return carry

    lax.fori_loop(0, n_blocks, out_phase, 0)


def _rec_mixer(layer, proj, batch, seq, conv_w, conv_b, w_gate, b_gate, lam, lb_logits, hg_norm):
    nh = A_HEADS
    hs = MIX_HEADS
    w = hs * LANES
    groups = nh // hs
    col = lambda part: pl.BlockSpec((seq, w), lambda b, h, part=part: (b, part * groups + h))
    in_specs = [col(0), col(1), col(2), col(3), col(4), col(5), col(6),
                pl.BlockSpec((CONV_WIDTH, w), lambda b, h: (0, h)),
                pl.BlockSpec((1, w), lambda b, h: (0, h)),
                pl.BlockSpec((hs, LANES, 4 * LANES), lambda b, h: (h, 0, 0)),
                pl.BlockSpec((hs, 1, 4 * LANES), lambda b, h: (h, 0, 0)),
                pl.BlockSpec((2, w), lambda b, h: (0, h)),
                pl.BlockSpec((2, N_REC_LAYERS, w), lambda b, h: (0, 0, h)),
                pl.BlockSpec((1, w), lambda b, h: (0, h)),
                pl.BlockSpec((2 * MIX_ROWS, MIX_ROWS), lambda b, h: (0, 0)),
                pl.BlockSpec((2, MIX_ROWS, MIX_ROWS), lambda b, h: (0, 0, 0))]
    seg_total = (seq // MIX_ROWS) * SEG_PITCH
    return pl.pallas_call(
        functools.partial(_rec_mixer_kernel, layer, seq),
        grid=(batch, groups),
        in_specs=in_specs,
        out_specs=pl.BlockSpec((seq, 2 * w), lambda b, h: (b, h)),
        out_shape=jax.ShapeDtypeStruct((batch * seq, D_A + D_B), BF16),
        scratch_shapes=[
            pltpu.VMEM((hs, seq + 2 * SUBLANES, LANES), F32),
            pltpu.VMEM((hs, 2, seg_total, LANES), F32),
            pltpu.VMEM((hs, 2, seg_total, LANES), F32),
            pltpu.VMEM((hs, 2, seg_total, LANES), F32),
            pltpu.VMEM((hs, 2, seg_total, LANES), F32),
            pltpu.VMEM((hs, 2, SUBLANES, LANES), F32),
            pltpu.VMEM((hs, 2, seq, LANES), F32),
            pltpu.VMEM((hs, 2, LANES, LANES), F32),
        ],
        compiler_params=pltpu.CompilerParams(
            dimension_semantics=("parallel", "parallel"), vmem_limit_bytes=VMEM_LIMIT),
        name="rec_mixer",
    )(proj, proj, proj, proj, proj, proj, proj,
      conv_w, conv_b.reshape(1, D_A), w_gate, b_gate, lam, lb_logits, hg_norm.reshape(1, D_B),
      _chunk_sum_matrix(), _chunk_causal_masks())


ATTN_QBLOCKS = 8
PREP_UNROLL = 4
VT_PAD = 16


def _attn_kernel(seq, q_ref, k_ref, v_ref, sink_ref, o_ref, ks_ref, vt_ref):
    n = pl.program_id(1)
    span = 3 * QBLOCK
    heads_per_tile = LANES // HEAD_DIM
    tiles_per_group = GROUP // heads_per_tile

    @pl.when(n == 0)
    def _():
        def prep(j, carry):
            r = pl.ds(pl.multiple_of(j * QBLOCK, QBLOCK), QBLOCK)
            upper = lax.broadcasted_iota(jnp.int32, (QBLOCK, LANES), 1) >= HEAD_DIM
            ones_tile = jnp.where(lax.broadcasted_iota(jnp.int32, (VT_PAD, QBLOCK), 0) == 0, 1.0, 0.0)
            for t in range(N_KV_HEADS // heads_per_tile):
                kt = k_ref[r, pl.ds(t * LANES, LANES)].astype(F32)
                vt = v_ref[r, pl.ds(t * LANES, LANES)].astype(F32).T
                for hh in range(heads_per_tile):
                    g = t * heads_per_tile + hh
                    k_own = jnp.where(upper if hh else jnp.logical_not(upper), kt, 0.0)
                    ks_ref[g, r, :] = (k_own + pltpu.roll(k_own, HEAD_DIM, 1)).astype(BF16)
                    vt_ref[g, j, pl.ds(0, HEAD_DIM), :] = vt[hh * HEAD_DIM:(hh + 1) * HEAD_DIM].astype(BF16)
                    vt_ref[g, j, pl.ds(HEAD_DIM, VT_PAD), :] = ones_tile.astype(BF16)
            return carry

        lax.fori_loop(0, seq // QBLOCK, prep, 0, unroll=PREP_UNROLL)

    cols = GROUP * QBLOCK
    col = lax.broadcasted_iota(jnp.int32, (1, cols), 1)
    upper = lax.broadcasted_iota(jnp.int32, (tiles_per_group * QBLOCK, LANES), 1) >= HEAD_DIM
    subs = range(ATTN_QBLOCKS)
    starts, biases = [], []
    for s in subs:
        q0 = pl.multiple_of((n * ATTN_QBLOCKS + s) * QBLOCK, QBLOCK)
        start = pl.multiple_of(jnp.clip(q0 - QBLOCK, 0, seq - span), QBLOCK)
        kpos = start + lax.broadcasted_iota(jnp.int32, (span, LANES), 0)
        qpos = q0 + lax.broadcasted_iota(jnp.int32, (span, LANES), 1)
        bias = jnp.where(jnp.abs(qpos - kpos) <= WINDOW, 0.0, -jnp.inf)
        starts.append(start)
        biases.append(jnp.concatenate([bias] * GROUP, axis=1))
    scores, weights = {}, {}
    for s in subs:
        q_rows = pl.ds(s * QBLOCK, QBLOCK)
        for g in range(N_KV_HEADS):
            qs = jnp.concatenate([q_ref[q_rows, pl.ds((g * tiles_per_group + t) * LANES, LANES)]
                                  for t in range(tiles_per_group)], axis=0)
            zero = jnp.zeros_like(qs)
            rhs = jnp.concatenate([jnp.where(upper, zero, qs), jnp.where(upper, qs, zero)], axis=0)
            sc = _dot_nt(ks_ref[g, pl.ds(starts[s], span), :], rhs) + biases[s]
            sink = sink_ref[g * GROUP]
            for p in range(heads_per_tile):
                for t in range(tiles_per_group):
                    if p or t:
                        sink = jnp.where(col >= (p * tiles_per_group + t) * QBLOCK,
                                         sink_ref[g * GROUP + t * heads_per_tile + p], sink)
            m = jnp.maximum(jnp.max(sc, axis=0, keepdims=True), sink)
            scores[s, g] = (sc, m, jnp.exp(sink - m))
    for s in subs:
        for g in range(N_KV_HEADS):
            sc, m, e_sink = scores[s, g]
            weights[s, g] = (jnp.exp((sc - m).astype(BF16)), e_sink)
    for s in subs:
        first_block = starts[s] // QBLOCK
        for g in range(N_KV_HEADS):
            e, e_sink = weights[s, g]
            pv = jnp.zeros((HEAD_DIM + VT_PAD, cols), F32)
            for j in range(span // QBLOCK):
                pv = pv + _dot(vt_ref[g, first_block + j], e[j * QBLOCK:(j + 1) * QBLOCK, :])
            out = pv[:HEAD_DIM] * (1.0 / (pv[HEAD_DIM:HEAD_DIM + 1] + e_sink))
            for t in range(tiles_per_group):
                tile = jnp.concatenate(
                    [out[:, (p * tiles_per_group + t) * QBLOCK:(p * tiles_per_group + t + 1) * QBLOCK]
                     for p in range(heads_per_tile)], axis=0)
                o_ref[pl.ds(s * QBLOCK, QBLOCK), pl.ds((g * tiles_per_group + t) * LANES, LANES)] = (
                    tile.T.astype(o_ref.dtype))


def _attention(qkv, batch, seq, sinks):
    nb = seq // QBLOCK
    steps = nb // ATTN_QBLOCKS
    q_w = N_Q_HEADS * HEAD_DIM
    kv_w = N_KV_HEADS * HEAD_DIM
    return pl.pallas_call(
        functools.partial(_attn_kernel, seq),
        grid=(batch, steps),
        in_specs=[
            pl.BlockSpec((ATTN_QBLOCKS * QBLOCK, q_w), lambda b, n: (b * steps + n, 0)),
            pl.BlockSpec((seq, kv_w), lambda b, n: (b, q_w // kv_w)),
            pl.BlockSpec((seq, kv_w), lambda b, n: (b, q_w // kv_w + 1)),
            pl.BlockSpec(memory_space=pltpu.SMEM),
        ],
        out_specs=pl.BlockSpec((ATTN_QBLOCKS * QBLOCK, q_w), lambda b, n: (b * steps + n, 0)),
        out_shape=jax.ShapeDtypeStruct((batch * seq, q_w), BF16),
        scratch_shapes=[pltpu.VMEM((N_KV_HEADS, seq, LANES), BF16),
                        pltpu.VMEM((N_KV_HEADS, nb, HEAD_DIM + VT_PAD, QBLOCK), BF16)],
        compiler_params=pltpu.CompilerParams(
            dimension_semantics=("parallel", "arbitrary"), vmem_limit_bytes=VMEM_LIMIT),
        name="window_attention",
    )(qkv, qkv, qkv, sinks)


def _rope_tables(seq):
    pos = jnp.arange(seq, dtype=F32)
    inv_freq = ROPE_THETA ** (-jnp.arange(0, HEAD_DIM, 2, dtype=F32) / HEAD_DIM)
    ang = pos[:, None] * inv_freq[None, :]
    cos, sin = jnp.cos(ang), jnp.sin(ang)
    reps = LANES // HEAD_DIM
    return (jnp.tile(jnp.concatenate([cos, cos], axis=1), (1, reps)),
            jnp.tile(jnp.concatenate([-sin, sin], axis=1), (1, reps)))


def kernel(x, norm_g, rec_w_in, rec_conv_w, rec_conv_b, rg_w_r, rg_b_r, rg_w_i, rg_b_i, rg_lambda,
           hgrn_lb_logits, hgrn_norm_g, rec_w_out, att_w_qkv, att_sinks, att_w_o, mlp_w1, mlp_w2):
    batch, seq, d = x.shape
    cos_t, sin_t = _rope_tables(seq)
    h = x.reshape(batch * seq, d)
    w_att_out = att_w_o.astype(BF16)
    w_in_all, w_qkv_all = rec_w_in, att_w_qkv
    w1_all, w2_all = mlp_w1, mlp_w2
    w_rec_out = (rec_w_out.reshape(N_REC_LAYERS, 2, A_HEADS, A_BLOCK, d).transpose(0, 2, 1, 3, 4)
                 .reshape(N_REC_LAYERS, D_A + D_B, d).astype(BF16))
    for layer in range(DEPTH):
        g = norm_g[layer]
        if layer % 2 == 0:
            r = layer // 2
            proj = _rec_proj(h, g[0], w_in_all, r, BF16)
            w_gate = jnp.concatenate([rg_w_r[r, 0], rg_w_i[r, 0], rg_w_r[r, 1], rg_w_i[r, 1]], axis=-1).astype(BF16)
            b_gate = jnp.stack([rg_b_r[r, 0], rg_b_i[r, 0], rg_b_r[r, 1], rg_b_i[r, 1]], axis=0)
            b_gate = b_gate.reshape(4, A_HEADS, A_BLOCK).transpose(1, 0, 2).reshape(A_HEADS, 1, 4 * A_BLOCK)
            y = _rec_mixer(r, proj, batch, seq, rec_conv_w[r], rec_conv_b[r], w_gate, b_gate, rg_lambda[r],
                           hgrn_lb_logits, hgrn_norm_g[r])
            w_out, o_idx = w_rec_out, r
        else:
            a = layer // 2
            qkv = _qkv_proj(h, g[0], w_qkv_all, a, cos_t, sin_t, seq)
            y = _attention(qkv, batch, seq, att_sinks[a])
            w_out, o_idx = w_att_out, a
        h = _out_mlp(y, w_out, o_idx, g[1], h, g[2], w1_all, w2_all, layer, g[3])
    return h.reshape(batch, seq, d)
```
